```python
import math
import jax, jax.numpy as jnp
from jax import lax
import numpy as np

D_MODEL = 2048
BATCH = 4
SEQ = 4096
DEPTH = 2

SSD_HEADS = 32
SSD_HEAD_DIM = 64
SSD_WIDTH = SSD_HEADS * SSD_HEAD_DIM
SSD_STATE = 128
SSD_GROUPS = 4
SSD_CONV = 4
SSD_CHUNK = 128
SSD_CONV_DIM = SSD_WIDTH + 2 * SSD_GROUPS * SSD_STATE
DT_MIN = 0.001
DT_MAX = 0.1

SB_HEADS = 16
SB_HEAD_DIM = 128
SB_WIDTH = SB_HEADS * SB_HEAD_DIM
SB_BLOCK = 128

IN_DIM = SSD_WIDTH + SSD_CONV_DIM + SSD_HEADS + 3 * SB_WIDTH
MIX_WIDTH = SSD_WIDTH + SB_WIDTH

POOL_WINDOWS = (2, 4, 8, 16)
POOL_GROUP = D_MODEL // len(POOL_WINDOWS)

D_FF = 4 * D_MODEL
EPS = 1e-6

kernel_name = 'hybrid_ssd_stickbreak_pool_trunk'


def rms_norm(x, g):
    xf = x.astype(jnp.float32)
    y = xf * lax.rsqrt(jnp.mean(xf * xf, axis=-1, keepdims=True) + EPS)
    return (y * g.astype(jnp.float32)).astype(x.dtype)


def causal_depthwise_conv(x, w, b):
    k, c = w.shape
    y = lax.conv_general_dilated(
        x, w[:, None, :].astype(x.dtype), window_strides=(1,),
        padding=[(k - 1, 0)], dimension_numbers=('NWC', 'WIO', 'NWC'),
        feature_group_count=c)
    return y + b.astype(x.dtype)


def ssd_scan(x, dt, a, bmat, cmat):
    f32 = jnp.float32
    bsz, t, h, p = x.shape
    g, n = bmat.shape[2], bmat.shape[3]
    r, l = h // g, SSD_CHUNK
    nc = t // l
    xdt = (x.astype(f32) * dt[..., None]).reshape(bsz, nc, l, g, r, p)
    da = (dt * a).reshape(bsz, nc, l, g, r).transpose(0, 3, 4, 1, 2)
    bc = bmat.astype(f32).reshape(bsz, nc, l, g, n)
    cc = cmat.astype(f32).reshape(bsz, nc, l, g, n)
    a_cs = jnp.cumsum(da, axis=-1)
    causal = jnp.tril(jnp.ones((l, l), dtype=bool))
    seg = a_cs[..., :, None] - a_cs[..., None, :]
    decay = jnp.exp(jnp.where(causal, seg, -jnp.inf))
    cb = jnp.einsum('bclgn,bcsgn->bgcls', cc, bc)
    y_diag = jnp.einsum('bgrcls,bcsgrp->bclgrp', decay * cb[:, :, None], xdt)
    decay_to_end = jnp.exp(a_cs[..., -1:] - a_cs)
    chunk_states = jnp.einsum('bcsgn,bgrcs,bcsgrp->bcgrpn', bc, decay_to_end, xdt)
    chunk_decay = jnp.exp(a_cs[..., -1])

    def step(state, inp):
        s_c, d_c = inp
        return state * d_c[..., None, None] + s_c, state

    init = jnp.zeros((bsz, g, r, p, n), f32)
    _, prev = lax.scan(step, init, (jnp.moveaxis(chunk_states, 1, 0),
                                    jnp.moveaxis(chunk_decay, -1, 0)))
    y_off = jnp.einsum('bclgn,cbgrpn->bclgrp', cc, prev) * \
        jnp.exp(a_cs).transpose(0, 3, 4, 1, 2)[..., None]
    return (y_diag + y_off).reshape(bsz, t, h, p)


def stick_breaking_attention(q, k, v):
    bsz, h, t, d = q.shape
    scale = d ** -0.5
    outs = []
    for i in range(t // SB_BLOCK):
        q0 = i * SB_BLOCK
        end = q0 + SB_BLOCK
        z = jnp.einsum('bhqd,bhkd->bhqk', q[:, :, q0:end], k[:, :, :end]).astype(jnp.float32) * scale
        qpos = q0 + jnp.arange(SB_BLOCK)[:, None]
        kpos = jnp.arange(end)[None, :]
        mask = kpos < qpos
        log_beta = jax.nn.log_sigmoid(z)
        log_keep = jnp.where(mask, jax.nn.log_sigmoid(-z), 0.0)
        between = lax.cumsum(log_keep, axis=3, reverse=True) - log_keep
        w = jnp.where(mask, jnp.exp(log_beta + between), 0.0)
        outs.append(jnp.einsum('bhqk,bhkd->bhqd', w.astype(v.dtype), v[:, :, :end]))
    return jnp.concatenate(outs, axis=2)


def hybrid_mixer(h, w_in, conv_w, conv_b, dt_bias, a_log, d_skip, out_norm, q_norm, k_norm, w_out):
    bsz, t, _ = h.shape
    proj = h @ w_in
    cuts = [SSD_WIDTH, SSD_WIDTH + SSD_CONV_DIM, SSD_WIDTH + SSD_CONV_DIM + SSD_HEADS,
            SSD_WIDTH + SSD_CONV_DIM + SSD_HEADS + SB_WIDTH,
            SSD_WIDTH + SSD_CONV_DIM + SSD_HEADS + 2 * SB_WIDTH]
    z, xbc, dt_raw, q, k, v = jnp.split(proj, cuts, axis=-1)

    xbc = jax.nn.silu(causal_depthwise_conv(xbc, conv_w, conv_b))
    xs, bm, cm = jnp.split(xbc, [SSD_WIDTH, SSD_WIDTH + SSD_GROUPS * SSD_STATE], axis=-1)
    xs = xs.reshape(bsz, t, SSD_HEADS, SSD_HEAD_DIM)
    bm = bm.reshape(bsz, t, SSD_GROUPS, SSD_STATE)
    cm = cm.reshape(bsz, t, SSD_GROUPS, SSD_STATE)
    dt = jax.nn.softplus(dt_raw.astype(jnp.float32) + dt_bias.astype(jnp.float32))
    a = -jnp.exp(a_log.astype(jnp.float32))
    y = ssd_scan(xs, dt, a, bm, cm) + d_skip.astype(jnp.float32)[:, None] * xs.astype(jnp.float32)
    gated = y.reshape(bsz, t, SSD_WIDTH) * jax.nn.silu(z.astype(jnp.float32))
    gsz = SSD_WIDTH // SSD_GROUPS
    y_ssd = rms_norm(gated.reshape(bsz, t, SSD_GROUPS, gsz),
                     out_norm.reshape(SSD_GROUPS, gsz)).reshape(bsz, t, SSD_WIDTH)

    def heads(u):
        return u.reshape(bsz, t, SB_HEADS, SB_HEAD_DIM).transpose(0, 2, 1, 3)
    qh = rms_norm(heads(q), q_norm)
    kh = rms_norm(heads(k), k_norm)
    y_sb = stick_breaking_attention(qh, kh, heads(v))
    y_sb = y_sb.transpose(0, 2, 1, 3).reshape(bsz, t, SB_WIDTH)

    merged = jnp.concatenate([y_ssd.astype(h.dtype), y_sb.astype(h.dtype)], axis=-1)
    return merged @ w_out


def multiscale_pool(h, w, b, scale):
    bsz, t, _ = h.shape
    hf = h.astype(jnp.float32)
    cs = jnp.cumsum(hf, axis=1)
    count = jnp.arange(1, t + 1, dtype=jnp.float32)[:, None]
    diffs = []
    for gi, win in enumerate(POOL_WINDOWS):
        sl = slice(gi * POOL_GROUP, (gi + 1) * POOL_GROUP)
        c = cs[..., sl]
        lagged = jnp.pad(c, ((0, 0), (win, 0), (0, 0)))[:, :t]
        mean = (c - lagged) / jnp.minimum(count, float(win))
        diffs.append(mean - hf[..., sl])
    d = jnp.stack(diffs, axis=2).astype(h.dtype)
    y = jnp.einsum('btgc,gcd->btgd', d, w).reshape(bsz, t, D_MODEL) + b
    return y * scale


def sq_relu_mlp(h, w_up, w_down):
    u = jax.nn.relu(h @ w_up)
    return (u * u) @ w_down


def setup_inputs(seed: int = 0) -> dict:
    key = jax.random.key(seed)
    ks = jax.random.split(key, 20)
    ne = (DEPTH + 1) // 2
    no = DEPTH // 2
    f32 = jnp.float32

    def normal(k, shape, s):
        return jax.random.normal(k, shape, f32) * s

    def gain(k, shape):
        return 1.0 + 0.02 * jax.random.normal(k, shape, f32)

    dt0 = jnp.exp(jax.random.uniform(ks[5], (ne, SSD_HEADS), f32, math.log(DT_MIN), math.log(DT_MAX)))
    return {
        'x': normal(ks[0], (BATCH, SEQ, D_MODEL), 1.0),
        'hyb_norm': gain(ks[1], (ne, D_MODEL)),
        'hyb_w_in': normal(ks[2], (ne, D_MODEL, IN_DIM), D_MODEL ** -0.5),
        'ssd_conv_w': normal(ks[3], (ne, SSD_CONV, SSD_CONV_DIM), SSD_CONV ** -0.5),
        'ssd_conv_b': normal(ks[4], (ne, SSD_CONV_DIM), 0.02),
        'ssd_dt_bias': dt0 + jnp.log(-jnp.expm1(-dt0)),
        'ssd_a_log': jnp.log(jax.random.uniform(ks[6], (ne, SSD_HEADS), f32, 1.0, 16.0)),
        'ssd_d': 1.0 + 0.1 * jax.random.normal(ks[7], (ne, SSD_HEADS), f32),
        'ssd_out_norm': gain(ks[8], (ne, SSD_WIDTH)),
        'sb_q_norm': gain(ks[9], (ne, SB_HEAD_DIM)),
        'sb_k_norm': gain(ks[10], (ne, SB_HEAD_DIM)),
        'hyb_w_out': normal(ks[11], (ne, MIX_WIDTH, D_MODEL), MIX_WIDTH ** -0.5),
        'pool_norm': gain(ks[12], (no, D_MODEL)),
        'pool_w': normal(ks[13], (no, len(POOL_WINDOWS), POOL_GROUP, POOL_GROUP), POOL_GROUP ** -0.5),
        'pool_b': normal(ks[14], (no, D_MODEL), 0.02),
        'pool_scale': gain(ks[15], (no, D_MODEL)),
        'mlp_norm': gain(ks[16], (DEPTH, D_MODEL)),
        'mlp_w_up': normal(ks[17], (DEPTH, D_MODEL, D_FF), D_MODEL ** -0.5),
        'mlp_w_down': normal(ks[18], (DEPTH, D_FF, D_MODEL), D_FF ** -0.5),
    }


def reference(x, hyb_norm, hyb_w_in, ssd_conv_w, ssd_conv_b, ssd_dt_bias, ssd_a_log, ssd_d,
              ssd_out_norm, sb_q_norm, sb_k_norm, hyb_w_out, pool_norm, pool_w, pool_b,
              pool_scale, mlp_norm, mlp_w_up, mlp_w_down):
    for layer in range(DEPTH):
        i = layer // 2
        if layer % 2 == 0:
            mix = hybrid_mixer(rms_norm(x, hyb_norm[i]), hyb_w_in[i], ssd_conv_w[i], ssd_conv_b[i],
                               ssd_dt_bias[i], ssd_a_log[i], ssd_d[i], ssd_out_norm[i],
                               sb_q_norm[i], sb_k_norm[i], hyb_w_out[i])
        else:
            mix = multiscale_pool(rms_norm(x, pool_norm[i]), pool_w[i], pool_b[i], pool_scale[i])
        x = x + mix.astype(x.dtype)
        x = x + sq_relu_mlp(rms_norm(x, mlp_norm[layer]), mlp_w_up[layer], mlp_w_down[layer]).astype(x.dtype)
    return x
```

```python
import functools

import jax
import jax.numpy as jnp
from jax import lax
from jax.experimental import pallas as pl
from jax.experimental.pallas import tpu as pltpu

F32 = jnp.float32
BF16 = jnp.bfloat16
EPS = 1e-6

SSD_HEADS = 32
SSD_HEAD_DIM = 64
SSD_STATE = 128
SSD_GROUPS = 4
SSD_CONV = 4
SSD_HEADS_PER_GROUP = SSD_HEADS // SSD_GROUPS
SSD_GROUP_WIDTH = SSD_HEADS_PER_GROUP * SSD_HEAD_DIM
SSD_WIDTH = SSD_HEADS * SSD_HEAD_DIM
SB_HEADS = 16
SB_HEAD_DIM = 128
SB_WIDTH = SB_HEADS * SB_HEAD_DIM
POOL_WINDOWS = (2, 4, 8, 16)
POOL_HALO = 16

LANE = 128
SUBLANE = 8
VMEM_LIMIT_BYTES = 56 * 1024 * 1024

SSD_CHUNK = 128
SB_KEYS = 128
SB_QUERIES = 256
INPROJ_TM, INPROJ_TN = 512, 1024
OUTPROJ_TM, OUTPROJ_TN = 512, 1024
MLP_TM, MLP_TF = 512, 1024
POOL_TM = 512


def _params(*semantics):
    return pltpu.CompilerParams(dimension_semantics=semantics, vmem_limit_bytes=VMEM_LIMIT_BYTES)


def _rms_scale(v):
    return lax.rsqrt(jnp.mean(v * v, axis=-1, keepdims=True) + EPS)


def _sigmoid(v):
    return 1.0 / (1.0 + jnp.exp(-v))


def _split3(v):
    hi = v.astype(BF16)
    r = v - hi.astype(F32)
    mid = r.astype(BF16)
    lo = (r - mid.astype(F32)).astype(BF16)
    return hi, mid, lo


def _dot(a, b):
    return jnp.dot(a, b, preferred_element_type=F32)


def _inproj_body(x_ref, g_ref, w_ref, wdt_ref, qg_ref, kg_ref, o_ref, dt_ref, xn_ref, *,
                 q_tiles, k_tiles, sb_scale):
    j = pl.program_id(1)

    @pl.when(j == 0)
    def _():
        x = x_ref[...]
        xn_ref[...] = ((x * _rms_scale(x)) * g_ref[...]).astype(BF16)
        dt_ref[...] = _dot(xn_ref[...], wdt_ref[...])

    acc = _dot(xn_ref[...], w_ref[...])
    heads_per_tile = acc.shape[1] // SB_HEAD_DIM

    def head_norm(gain):
        for c in range(heads_per_tile):
            a = acc[:, c * SB_HEAD_DIM:(c + 1) * SB_HEAD_DIM]
            o_ref[:, c * SB_HEAD_DIM:(c + 1) * SB_HEAD_DIM] = ((a * _rms_scale(a)) * gain).astype(BF16)

    is_q = jnp.logical_and(j >= q_tiles[0], j < q_tiles[1])
    is_k = jnp.logical_and(j >= k_tiles[0], j < k_tiles[1])

    @pl.when(is_q)
    def _():
        head_norm(qg_ref[...] * sb_scale)

    @pl.when(is_k)
    def _():
        head_norm(kg_ref[...])

    @pl.when(jnp.logical_not(jnp.logical_or(is_q, is_k)))
    def _():
        o_ref[...] = acc.astype(BF16)


def _inproj(xf, gain, w_main, w_dt, q_gain, k_gain, q_tiles, k_tiles):
    n, d = xf.shape
    cols = w_main.shape[1]
    tm, tn = INPROJ_TM, INPROJ_TN
    body = functools.partial(_inproj_body, q_tiles=q_tiles, k_tiles=k_tiles,
                             sb_scale=SB_HEAD_DIM ** -0.5)
    return pl.pallas_call(
        body,
        grid=(n // tm, cols // tn),
        in_specs=[
            pl.BlockSpec((tm, d), lambda i, j: (i, 0)),
            pl.BlockSpec((1, d), lambda i, j: (0, 0)),
            pl.BlockSpec((d, tn), lambda i, j: (0, j)),
            pl.BlockSpec((d, w_dt.shape[1]), lambda i, j: (0, 0)),
            pl.BlockSpec((1, SB_HEAD_DIM), lambda i, j: (0, 0)),
            pl.BlockSpec((1, SB_HEAD_DIM), lambda i, j: (0, 0)),
        ],
        out_specs=[
            pl.BlockSpec((tm, tn), lambda i, j: (i, j)),
            pl.BlockSpec((tm, w_dt.shape[1]), lambda i, j: (i, 0)),
        ],
        out_shape=[
            jax.ShapeDtypeStruct((n, cols), BF16),
            jax.ShapeDtypeStruct((n, w_dt.shape[1]), F32),
        ],
        scratch_shapes=[pltpu.VMEM((tm, d), BF16)],
        compiler_params=_params("parallel", "arbitrary"),
        name="inproj",
    )(xf, gain, w_main, w_dt, q_gain, k_gain)


def _ssd_body(z_ref, x_ref, b_ref, c_ref, dt_ref, cwx_ref, cwb_ref, cwc_ref, cbx_ref, cbb_ref, cbc_ref,
              dtb_ref, alog_ref, dsk_ref, onorm_ref, tril_ref, e_ref, o_ref,
              state_ref, tx_ref, tb_ref, tc_ref):
    L = SSD_CHUNK

    @pl.when(pl.program_id(2) == 0)
    def _():
        state_ref[...] = jnp.zeros_like(state_ref)
        tx_ref[...] = jnp.zeros_like(tx_ref)
        tb_ref[...] = jnp.zeros_like(tb_ref)
        tc_ref[...] = jnp.zeros_like(tc_ref)

    def conv_silu(cur_ref, tail_ref, w_ref, bias_ref):
        cur = cur_ref[...].astype(F32)
        ext = jnp.concatenate([tail_ref[...], cur], axis=0)
        w = w_ref[...]
        y = bias_ref[...]
        for k in range(SSD_CONV):
            off = SUBLANE - (SSD_CONV - 1) + k
            y = y + w[k:k + 1, :] * ext[off:off + L, :]
        tail_ref[...] = cur[L - SUBLANE:, :]
        return y * _sigmoid(y)

    xs = conv_silu(x_ref, tx_ref, cwx_ref, cbx_ref)
    bm = conv_silu(b_ref, tb_ref, cwb_ref, cbb_ref).astype(BF16)
    cm = conv_silu(c_ref, tc_ref, cwc_ref, cbc_ref).astype(BF16)

    pre = dt_ref[...] + dtb_ref[0:1, :]
    dt = jnp.maximum(pre, 0.0) + jnp.log(1.0 + jnp.exp(-jnp.abs(pre)))
    da = dt * (-jnp.exp(alog_ref[0:1, :]))
    tril = tril_ref[...]
    cs = sum(_dot(tril, p) for p in _split3(da))
    cs_t = cs.T

    e = e_ref[...]
    both = jnp.concatenate([dt, cs], axis=0)
    both_e = sum(_dot(p, e) for p in _split3(both))
    dt_e, cs_e = both_e[:L], both_e[L:]
    cs_last = cs_e[L - 1:L, :]

    xdt = xs * dt_e
    xdt_b = xdt.astype(BF16)
    cb = lax.dot_general(cm, bm, (((1,), (1,)), ((), ())), preferred_element_type=F32)

    row = lax.broadcasted_iota(jnp.int32, (L, L), 0)
    col = lax.broadcasted_iota(jnp.int32, (L, L), 1)
    causal = row >= col
    lane = lax.broadcasted_iota(jnp.int32, (L, LANE), 1)
    ys = []
    for pr in range(SSD_HEADS_PER_GROUP // 2):
        ms = []
        for h in (2 * pr, 2 * pr + 1):
            seg = cs[:, h:h + 1] - cs_t[h:h + 1, :]
            dec = jnp.exp(jnp.where(causal, seg, -jnp.inf))
            ms.append((dec * cb).astype(BF16))
        xt = xdt_b[:, pr * LANE:(pr + 1) * LANE]
        zero = jnp.zeros_like(xt)
        rhs = jnp.concatenate([jnp.where(lane < SSD_HEAD_DIM, xt, zero),
                               jnp.where(lane >= SSD_HEAD_DIM, xt, zero)], axis=0)
        ys.append(_dot(jnp.concatenate(ms, axis=1), rhs))
    y_diag = jnp.concatenate(ys, axis=1)

    st = state_ref[...]
    y_off = _dot(cm, st.astype(BF16)) * jnp.exp(cs_e)
    xw = (xdt * jnp.exp(cs_last - cs_e)).astype(BF16)
    upd = lax.dot_general(bm, xw, (((0,), (0,)), ((), ())), preferred_element_type=F32)
    state_ref[...] = st * jnp.exp(cs_last) + upd

    y = y_diag + y_off + dsk_ref[...] * xs
    zf = z_ref[...].astype(F32)
    gated = y * (zf * _sigmoid(zf))
    o_ref[...] = ((gated * _rms_scale(gated)) * onorm_ref[...]).astype(BF16)


def _ssd(proj, dt_raw, conv_w, conv_b, dtb, alog, dskip_e, onorm, batch, seq):
    n = proj.shape[0]
    L = SSD_CHUNK
    nc = seq // L
    gw, st = SSD_GROUP_WIDTH, SSD_STATE
    x_blk = SSD_WIDTH // gw
    b_blk = 2 * SSD_WIDTH // st
    c_blk = b_blk + SSD_GROUPS
    cb_blk = SSD_WIDTH // st
    cc_blk = cb_blk + SSD_GROUPS
    tril = (lax.broadcasted_iota(jnp.int32, (L, L), 0) >= lax.broadcasted_iota(jnp.int32, (L, L), 1)).astype(BF16)
    expand = (lax.broadcasted_iota(jnp.int32, (LANE, gw), 0)
              == lax.broadcasted_iota(jnp.int32, (LANE, gw), 1) // SSD_HEAD_DIM).astype(BF16)
    rows = lambda b, g, c: b * nc + c
    return pl.pallas_call(
        _ssd_body,
        grid=(batch, SSD_GROUPS, nc),
        in_specs=[
            pl.BlockSpec((L, gw), lambda b, g, c: (rows(b, g, c), g)),
            pl.BlockSpec((L, gw), lambda b, g, c: (rows(b, g, c), x_blk + g)),
            pl.BlockSpec((L, st), lambda b, g, c: (rows(b, g, c), b_blk + g)),
            pl.BlockSpec((L, st), lambda b, g, c: (rows(b, g, c), c_blk + g)),
            pl.BlockSpec((L, LANE), lambda b, g, c: (rows(b, g, c), g)),
            pl.BlockSpec((SSD_CONV, gw), lambda b, g, c: (0, g)),
            pl.BlockSpec((SSD_CONV, st), lambda b, g, c: (0, cb_blk + g)),
            pl.BlockSpec((SSD_CONV, st), lambda b, g, c: (0, cc_blk + g)),
            pl.BlockSpec((1, gw), lambda b, g, c: (0, g)),
            pl.BlockSpec((1, st), lambda b, g, c: (0, cb_blk + g)),
            pl.BlockSpec((1, st), lambda b, g, c: (0, cc_blk + g)),
            pl.BlockSpec((SUBLANE, LANE), lambda b, g, c: (g, 0)),
            pl.BlockSpec((SUBLANE, LANE), lambda b, g, c: (g, 0)),
            pl.BlockSpec((1, gw), lambda b, g, c: (0, g)),
            pl.BlockSpec((1, gw), lambda b, g, c: (0, g)),
            pl.BlockSpec((L, L), lambda b, g, c: (0, 0)),
            pl.BlockSpec((LANE, gw), lambda b, g, c: (0, 0)),
        ],
        out_specs=pl.BlockSpec((L, gw), lambda b, g, c: (rows(b, g, c), g)),
        out_shape=jax.ShapeDtypeStruct((n, SSD_WIDTH), BF16),
        scratch_shapes=[
            pltpu.VMEM((st, gw), F32),
            pltpu.VMEM((SUBLANE, gw), F32),
            pltpu.VMEM((SUBLANE, st), F32),
            pltpu.VMEM((SUBLANE, st), F32),
        ],
        compiler_params=_params("parallel", "parallel", "arbitrary"),
        name="ssd",
    )(proj, proj, proj, proj, dt_raw, conv_w, conv_w, conv_w, conv_b, conv_b, conv_b,
      dtb, alog, dskip_e, onorm, tril, expand)


def _sb_body(q_ref, k_ref, v_ref, uu_ref, o_ref):
    tq, kb = SB_QUERIES, SB_KEYS
    nd = tq // kb
    qi = pl.program_id(2)
    q = q_ref[...]
    uu = uu_ref[...]
    row = lax.broadcasted_iota(jnp.int32, (tq, kb), 0)
    lane = lax.broadcasted_iota(jnp.int32, (tq, kb), 1)

    def tile(kstart, acc, carry, mask):
        k = k_ref[pl.ds(kstart, kb), :]
        v = v_ref[pl.ds(kstart, kb), :]
        z = lax.dot_general(q, k, (((1,), (1,)), ((), ())), preferred_element_type=F32)
        sp = jnp.log(1.0 + jnp.exp(-jnp.abs(z)))
        log_beta = jnp.minimum(z, 0.0) - sp
        log_keep = log_beta - z
        if mask is not None:
            log_keep = jnp.where(mask, log_keep, 0.0)
        hi = log_keep.astype(BF16)
        lo = (log_keep - hi.astype(F32)).astype(BF16)
        sums = _dot(jnp.concatenate([hi, lo], axis=1), uu)
        w = jnp.exp(log_beta + sums[:, :kb] + carry)
        if mask is not None:
            w = jnp.where(mask, w, 0.0)
        acc = acc + _dot(w.astype(BF16), v)
        return acc, carry + sums[:, kb:]

    acc = jnp.zeros((tq, SB_HEAD_DIM), F32)
    carry = jnp.zeros((tq, kb), F32)
    q0 = qi * tq
    for d in reversed(range(nd)):
        acc, carry = tile(pl.multiple_of(q0 + d * kb, kb), acc, carry, (d * kb + lane) < row)

    def body(i, ac):
        kstart = pl.multiple_of((qi * nd - 1 - i) * kb, kb)
        return tile(kstart, ac[0], ac[1], None)

    acc, carry = lax.fori_loop(0, qi * nd, body, (acc, carry))
    o_ref[...] = acc.astype(BF16)


def _sb_attention(proj, batch, seq, q_blk, k_blk, v_blk):
    n = proj.shape[0]
    tq, kb, hd = SB_QUERIES, SB_KEYS, SB_HEAD_DIM
    nq = seq // tq
    r = lax.broadcasted_iota(jnp.int32, (2 * kb, 2 * kb), 0) % kb
    c = lax.broadcasted_iota(jnp.int32, (2 * kb, 2 * kb), 1)
    uu = jnp.logical_or(c >= kb, r > c).astype(BF16)
    return pl.pallas_call(
        _sb_body,
        grid=(batch, SB_HEADS, nq),
        in_specs=[
            pl.BlockSpec((tq, hd), lambda b, h, i: (b * nq + i, q_blk + h)),
            pl.BlockSpec((seq, hd), lambda b, h, i: (b, k_blk + h)),
            pl.BlockSpec((seq, hd), lambda b, h, i: (b, v_blk + h)),
            pl.BlockSpec((2 * kb, 2 * kb), lambda b, h, i: (0, 0)),
        ],
        out_specs=pl.BlockSpec((tq, hd), lambda b, h, i: (b * nq + i, h)),
        out_shape=jax.ShapeDtypeStruct((n, SB_WIDTH), BF16),
        compiler_params=_params("parallel", "parallel", "arbitrary"),
        name="sb_attention",
    )(proj, proj, proj, uu)


def _outproj_body(ya_ref, yb_ref, wa_ref, wb_ref, x_ref, o_ref):
    o_ref[...] = x_ref[...] + _dot(ya_ref[...], wa_ref[...]) + _dot(yb_ref[...], wb_ref[...])


def _outproj(y_ssd, y_sb, w_a, w_b, xf):
    n, d = xf.shape
    tm, tn = OUTPROJ_TM, OUTPROJ_TN
    ka, kb = y_ssd.shape[1], y_sb.shape[1]
    return pl.pallas_call(
        _outproj_body,
        grid=(d // tn, n // tm),
        in_specs=[
            pl.BlockSpec((tm, ka), lambda j, i: (i, 0)),
            pl.BlockSpec((tm, kb), lambda j, i: (i, 0)),
            pl.BlockSpec((ka, tn), lambda j, i: (0, j)),
            pl.BlockSpec((kb, tn), lambda j, i: (0, j)),
            pl.BlockSpec((tm, tn), lambda j, i: (i, j)),
        ],
        out_specs=pl.BlockSpec((tm, tn), lambda j, i: (i, j)),
        out_shape=jax.ShapeDtypeStruct((n, d), F32),
        compiler_params=_params("parallel", "parallel"),
        name="outproj",
    )(y_ssd, y_sb, w_a, w_b, xf)


def _mlp_body(x_ref, g_ref, wu_ref, wd_ref, o_ref, xn_ref):
    f = pl.program_id(1)

    @pl.when(f == 0)
    def _():
        x = x_ref[...]
        xn_ref[...] = ((x * _rms_scale(x)) * g_ref[...]).astype(BF16)
        o_ref[...] = x

    u = jnp.maximum(_dot(xn_ref[...], wu_ref[...]), 0.0)
    o_ref[...] += _dot((u * u).astype(BF16), wd_ref[...])


def _mlp(xf, gain, w_up, w_down):
    n, d = xf.shape
    ff = w_up.shape[1]
    tm, tf = MLP_TM, MLP_TF
    return pl.pallas_call(
        _mlp_body,
        grid=(n // tm, ff // tf),
        in_specs=[
            pl.BlockSpec((tm, d), lambda i, f: (i, 0)),
            pl.BlockSpec((1, d), lambda i, f: (0, 0)),
            pl.BlockSpec((d, tf), lambda i, f: (0, f)),
            pl.BlockSpec((tf, d), lambda i, f: (f, 0)),
        ],
        out_specs=pl.BlockSpec((tm, d), lambda i, f: (i, 0)),
        out_shape=jax.ShapeDtypeStruct((n, d), F32),
        scratch_shapes=[pltpu.VMEM((tm, d), BF16)],
        compiler_params=_params("parallel", "arbitrary"),
        name="mlp",
    )(xf, gain, w_up, w_down)


def _pool_body(x_ref, halo_ref, g_ref, w_ref, b_ref, s_ref, o_ref, *, seq):
    tm = x_ref.shape[0]
    gch = w_ref.shape[1]
    t0 = (pl.program_id(0) * tm) % seq
    x = x_ref[...]
    gain = g_ref[...]
    h = (x * _rms_scale(x)) * gain
    halo = halo_ref[...]
    hh = (halo * _rms_scale(halo)) * gain
    hh = jnp.where(t0 == 0, 0.0, hh)
    ext = jnp.concatenate([hh, h], axis=0)
    pos = t0 + lax.broadcasted_iota(jnp.int32, (tm, 1), 0)
    for gi, win in enumerate(POOL_WINDOWS):
        sl = slice(gi * gch, (gi + 1) * gch)
        s = ext[:, sl]
        off, width = 0, 1
        while width < win:
            s = s[width:, :] + s[:-width, :]
            off += width
            width *= 2
        s = s[POOL_HALO - off:POOL_HALO - off + tm, :]
        count = jnp.minimum(pos + 1, win).astype(F32)
        dgrp = (s / count - h[:, sl]).astype(BF16)
        y = _dot(dgrp, w_ref[gi]) + b_ref[:, sl]
        o_ref[:, sl] = x[:, sl] + y * s_ref[:, sl]


def _pool(xf, gain, w, b, scale, seq):
    n, d = xf.shape
    tm = POOL_TM
    per = tm // POOL_HALO
    return pl.pallas_call(
        functools.partial(_pool_body, seq=seq),
        grid=(n // tm,),
        in_specs=[
            pl.BlockSpec((tm, d), lambda i: (i, 0)),
            pl.BlockSpec((POOL_HALO, d), lambda i: (jnp.maximum(i * per - 1, 0), 0)),
            pl.BlockSpec((1, d), lambda i: (0, 0)),
            pl.BlockSpec(w.shape, lambda i: (0, 0, 0)),
            pl.BlockSpec((1, d), lambda i: (0, 0)),
            pl.BlockSpec((1, d), lambda i: (0, 0)),
        ],
        out_specs=pl.BlockSpec((tm, d), lambda i: (i, 0)),
        out_shape=jax.ShapeDtypeStruct((n, d), F32),
        compiler_params=_params("parallel"),
        name="pool",
    )(xf, xf, gain, w, b, scale)


def _hybrid_layer(xf, batch, seq, norm, w_in, conv_w, conv_b, dt_bias, a_log, d_skip, out_norm,
                  q_norm, k_norm, w_out):
    d = xf.shape[1]
    conv_dim = conv_w.shape[1]
    dt_lo = SSD_WIDTH + conv_dim
    dt_hi = dt_lo + SSD_HEADS
    w_main = jnp.concatenate([w_in[:, :dt_lo], w_in[:, dt_hi:]], axis=1).astype(BF16)
    hpg = SSD_HEADS_PER_GROUP

    def per_group(v):
        v = v.reshape(v.shape[:-1] + (SSD_GROUPS, hpg))
        return jnp.pad(v, [(0, 0)] * (v.ndim - 1) + [(0, LANE - hpg)])

    w_dt = per_group(w_in[:, dt_lo:dt_hi]).reshape(d, SSD_GROUPS * LANE).astype(BF16)

    def group_rows(v):
        v = per_group(v.astype(F32))[:, None, :]
        return jnp.pad(v, [(0, 0), (0, SUBLANE - 1), (0, 0)]).reshape(SSD_GROUPS * SUBLANE, LANE)

    q_lo = dt_lo // INPROJ_TN
    sb_tiles = SB_WIDTH // INPROJ_TN
    proj, dt_raw = _inproj(xf, norm.reshape(1, d), w_main, w_dt,
                           q_norm.reshape(1, -1), k_norm.reshape(1, -1),
                           (q_lo, q_lo + sb_tiles), (q_lo + sb_tiles, q_lo + 2 * sb_tiles))
    y_ssd = _ssd(proj, dt_raw, conv_w, conv_b.reshape(1, -1), group_rows(dt_bias), group_rows(a_log),
                 jnp.repeat(d_skip.astype(F32), SSD_HEAD_DIM).reshape(1, -1), out_norm.reshape(1, -1),
                 batch, seq)
    q_blk = dt_lo // SB_HEAD_DIM
    y_sb = _sb_attention(proj, batch, seq, q_blk, q_blk + SB_HEADS, q_blk + 2 * SB_HEADS)
    w_out_b = w_out.astype(BF16)
    return _outproj(y_ssd, y_sb, w_out_b[:SSD_WIDTH], w_out_b[SSD_WIDTH:], xf)


def kernel(x, hyb_norm, hyb_w_in, ssd_conv_w, ssd_conv_b, ssd_dt_bias, ssd_a_log, ssd_d, ssd_out_norm, sb_q_norm, sb_k_norm, hyb_w_out, pool_norm, pool_w, pool_b, pool_scale, mlp_norm, mlp_w_up, mlp_w_down):
    batch, seq, d = x.shape
    xf = x.reshape(batch * seq, d)
    for layer in range(mlp_norm.shape[0]):
        i = layer // 2
        if layer % 2 == 0:
            xf = _hybrid_layer(xf, batch, seq, hyb_norm[i], hyb_w_in[i], ssd_conv_w[i], ssd_conv_b[i],
                               ssd_dt_bias[i], ssd_a_log[i], ssd_d[i], ssd_out_norm[i],
                               sb_q_norm[i], sb_k_norm[i], hyb_w_out[i])
        else:
            xf = _pool(xf, pool_norm[i].reshape(1, d), pool_w[i].astype(BF16), pool_b[i].reshape(1, d),
                       pool_scale[i].reshape(1, d), seq)
        xf = _mlp(xf, mlp_norm[layer].reshape(1, d), mlp_w_up[layer].astype(BF16),
                  mlp_w_down[layer].astype(BF16))
    return xf.reshape(batch, seq, d)
```

```python
import functools

import jax
import jax.numpy as jnp
from jax import lax
from jax.experimental import pallas as pl
from jax.experimental.pallas import tpu as pltpu

F32 = jnp.float32
BF16 = jnp.bfloat16
EPS = 1e-6
LOG2E = 1.4426950408889634
MASKED_LOG = -1e30

SSD_HEADS = 32
SSD_HEAD_DIM = 64
SSD_STATE = 128
SSD_GROUPS = 4
SSD_CONV = 4
SSD_HEADS_PER_GROUP = SSD_HEADS // SSD_GROUPS
SSD_GROUP_WIDTH = SSD_HEADS_PER_GROUP * SSD_HEAD_DIM
SSD_WIDTH = SSD_HEADS * SSD_HEAD_DIM
SB_HEADS = 16
SB_HEAD_DIM = 128
SB_WIDTH = SB_HEADS * SB_HEAD_DIM
POOL_WINDOWS = (2, 4, 8, 16)
POOL_HALO = 16

LANE = 128
SUBLANE = 8
VMEM_LIMIT_BYTES = 56 * 1024 * 1024

SSD_CHUNK = 128
SB_KEYS = 128
SB_QUERIES = 512
INPROJ_TM, INPROJ_TN = 512, 1024
OUTPROJ_TM, OUTPROJ_TN = 512, 1024
MLP_TM, MLP_TF = 512, 1024
POOL_TM = 512


def _params(*semantics):
    return pltpu.CompilerParams(dimension_semantics=semantics, vmem_limit_bytes=VMEM_LIMIT_BYTES)


def _rms_scale(v):
    return lax.rsqrt(jnp.mean(v * v, axis=-1, keepdims=True) + EPS)


def _silu(v):
    h = 0.5 * v
    return h + h * jnp.tanh(h)


def _split3(v):
    hi = v.astype(BF16)
    r = v - hi.astype(F32)
    mid = r.astype(BF16)
    lo = (r - mid.astype(F32)).astype(BF16)
    return hi, mid, lo


def _dot(a, b):
    return jnp.dot(a, b, preferred_element_type=F32)


def _inproj_body(x_ref, g_ref, w_ref, wdt_ref, qg_ref, kg_ref, o_ref, dt_ref, xn_ref, *,
                 q_tiles, k_tiles, sb_scale):
    j = pl.program_id(1)

    @pl.when(j == 0)
    def _():
        x = x_ref[...]
        xn_ref[...] = ((x * _rms_scale(x)) * g_ref[...]).astype(BF16)
        dt_ref[...] = _dot(xn_ref[...], wdt_ref[...])

    acc = _dot(xn_ref[...], w_ref[...])
    heads_per_tile = acc.shape[1] // SB_HEAD_DIM

    def head_norm(gain):
        for c in range(heads_per_tile):
            a = acc[:, c * SB_HEAD_DIM:(c + 1) * SB_HEAD_DIM]
            o_ref[:, c * SB_HEAD_DIM:(c + 1) * SB_HEAD_DIM] = ((a * _rms_scale(a)) * gain).astype(BF16)

    is_q = jnp.logical_and(j >= q_tiles[0], j < q_tiles[1])
    is_k = jnp.logical_and(j >= k_tiles[0], j < k_tiles[1])

    @pl.when(is_q)
    def _():
        head_norm(qg_ref[...] * sb_scale)

    @pl.when(is_k)
    def _():
        head_norm(kg_ref[...])

    @pl.when(jnp.logical_not(jnp.logical_or(is_q, is_k)))
    def _():
        o_ref[...] = acc.astype(BF16)


def _inproj(xf, gain, w_main, w_dt, q_gain, k_gain, q_tiles, k_tiles):
    n, d = xf.shape
    cols = w_main.shape[1]
    tm, tn = INPROJ_TM, INPROJ_TN
    body = functools.partial(_inproj_body, q_tiles=q_tiles, k_tiles=k_tiles,
                             sb_scale=SB_HEAD_DIM ** -0.5)
    return pl.pallas_call(
        body,
        grid=(n // tm, cols // tn),
        in_specs=[
            pl.BlockSpec((tm, d), lambda i, j: (i, 0)),
            pl.BlockSpec((1, d), lambda i, j: (0, 0)),
            pl.BlockSpec((d, tn), lambda i, j: (0, j)),
            pl.BlockSpec((d, w_dt.shape[1]), lambda i, j: (0, 0)),
            pl.BlockSpec((1, SB_HEAD_DIM), lambda i, j: (0, 0)),
            pl.BlockSpec((1, SB_HEAD_DIM), lambda i, j: (0, 0)),
        ],
        out_specs=[
            pl.BlockSpec((tm, tn), lambda i, j: (i, j)),
            pl.BlockSpec((tm, w_dt.shape[1]), lambda i, j: (i, 0)),
        ],
        out_shape=[
            jax.ShapeDtypeStruct((n, cols), BF16),
            jax.ShapeDtypeStruct((n, w_dt.shape[1]), F32),
        ],
        scratch_shapes=[pltpu.VMEM((tm, d), BF16)],
        compiler_params=_params("parallel", "arbitrary"),
        name="inproj",
    )(xf, gain, w_main, w_dt, q_gain, k_gain)


def _ssd_body(z_ref, x_ref, b_ref, c_ref, dt_ref, cwx_ref, cwb_ref, cwc_ref, cbx_ref, cbb_ref, cbc_ref,
              dtb_ref, alog_ref, dsk_ref, onorm_ref, tril_ref, e_ref, shift_ref, o_ref,
              state_ref, prev_ref):
    L = SSD_CHUNK
    gw, st = SSD_GROUP_WIDTH, SSD_STATE

    step = pl.program_id(2)

    @pl.when(step == 0)
    def _():
        state_ref[...] = jnp.zeros_like(state_ref)
        prev_ref[0] = jnp.zeros(prev_ref.shape[1:], prev_ref.dtype)

    cur = jnp.concatenate([x_ref[...], b_ref[...], c_ref[...]], axis=1)
    shifted = _dot(shift_ref[...], jnp.concatenate([prev_ref[step % 2], cur], axis=0))
    prev_ref[(step + 1) % 2] = cur

    def conv_silu(lo, hi, w_ref, bias_ref):
        w = w_ref[...]
        y = bias_ref[...] + w[SSD_CONV - 1:SSD_CONV, :] * cur[:, lo:hi].astype(F32)
        for k in range(SSD_CONV - 1):
            y = y + w[k:k + 1, :] * shifted[k * L:(k + 1) * L, lo:hi]
        return _silu(y)

    xs = conv_silu(0, gw, cwx_ref, cbx_ref)
    bm = conv_silu(gw, gw + st, cwb_ref, cbb_ref).astype(BF16)
    cm = conv_silu(gw + st, gw + 2 * st, cwc_ref, cbc_ref).astype(BF16)

    pre = dt_ref[...] + dtb_ref[0:1, :]
    dt = jnp.maximum(pre, 0.0) + jnp.log(1.0 + jnp.exp(-jnp.abs(pre)))
    da = dt * (-jnp.exp(alog_ref[0:1, :]))
    tril = tril_ref[...]
    cs = sum(_dot(tril, p) for p in _split3(da))
    cs_t = cs.T

    e = e_ref[...]
    both = jnp.concatenate([dt, cs], axis=0)
    both_e = sum(_dot(p, e) for p in _split3(both))
    dt_e, cs_e = both_e[:L], both_e[L:]
    cs_last = cs_e[L - 1:L, :]

    xdt = xs * dt_e
    xdt_b = xdt.astype(BF16)
    cb = lax.dot_general(cm, bm, (((1,), (1,)), ((), ())), preferred_element_type=F32)

    row = lax.broadcasted_iota(jnp.int32, (L, L), 0)
    col = lax.broadcasted_iota(jnp.int32, (L, L), 1)
    causal = row >= col
    lane = lax.broadcasted_iota(jnp.int32, (L, LANE), 1)
    ys = []
    for pr in range(SSD_HEADS_PER_GROUP // 2):
        ms = []
        for h in (2 * pr, 2 * pr + 1):
            seg = cs[:, h:h + 1] - cs_t[h:h + 1, :]
            dec = jnp.exp(jnp.where(causal, seg, -jnp.inf))
            ms.append((dec * cb).astype(BF16))
        xt = xdt_b[:, pr * LANE:(pr + 1) * LANE]
        zero = jnp.zeros_like(xt)
        rhs = jnp.concatenate([jnp.where(lane < SSD_HEAD_DIM, xt, zero),
                               jnp.where(lane >= SSD_HEAD_DIM, xt, zero)], axis=0)
        ys.append(_dot(jnp.concatenate(ms, axis=1), rhs))
    y_diag = jnp.concatenate(ys, axis=1)

    st = state_ref[...]
    y_off = _dot(cm, st.astype(BF16)) * jnp.exp(cs_e)
    xw = (xdt * jnp.exp(cs_last - cs_e)).astype(BF16)
    upd = lax.dot_general(bm, xw, (((0,), (0,)), ((), ())), preferred_element_type=F32)
    state_ref[...] = st * jnp.exp(cs_last) + upd

    y = y_diag + y_off + dsk_ref[...] * xs
    zf = z_ref[...].astype(F32)
    gated = y * _silu(zf)
    o_ref[...] = ((gated * _rms_scale(gated)) * onorm_ref[...]).astype(BF16)


def _ssd(proj, dt_raw, conv_w, conv_b, dtb, alog, dskip_e, onorm, batch, seq):
    n = proj.shape[0]
    L = SSD_CHUNK
    nc = seq // L
    gw, st = SSD_GROUP_WIDTH, SSD_STATE
    x_blk = SSD_WIDTH // gw
    b_blk = 2 * SSD_WIDTH // st
    c_blk = b_blk + SSD_GROUPS
    cb_blk = SSD_WIDTH // st
    cc_blk = cb_blk + SSD_GROUPS
    tril = (lax.broadcasted_iota(jnp.int32, (L, L), 0) >= lax.broadcasted_iota(jnp.int32, (L, L), 1)).astype(BF16)
    expand = (lax.broadcasted_iota(jnp.int32, (LANE, gw), 0)
              == lax.broadcasted_iota(jnp.int32, (LANE, gw), 1) // SSD_HEAD_DIM).astype(BF16)
    taps = SSD_CONV - 1
    r = lax.broadcasted_iota(jnp.int32, (taps * L, 2 * L), 0)
    shift = (lax.broadcasted_iota(jnp.int32, (taps * L, 2 * L), 1) == L + r % L - taps + r // L).astype(BF16)
    rows = lambda b, g, c: b * nc + c
    return pl.pallas_call(
        _ssd_body,
        grid=(batch, SSD_GROUPS, nc),
        in_specs=[
            pl.BlockSpec((L, gw), lambda b, g, c: (rows(b, g, c), g)),
            pl.BlockSpec((L, gw), lambda b, g, c: (rows(b, g, c), x_blk + g)),
            pl.BlockSpec((L, st), lambda b, g, c: (rows(b, g, c), b_blk + g)),
            pl.BlockSpec((L, st), lambda b, g, c: (rows(b, g, c), c_blk + g)),
            pl.BlockSpec((L, LANE), lambda b, g, c: (rows(b, g, c), g)),
            pl.BlockSpec((SSD_CONV, gw), lambda b, g, c: (0, g)),
            pl.BlockSpec((SSD_CONV, st), lambda b, g, c: (0, cb_blk + g)),
            pl.BlockSpec((SSD_CONV, st), lambda b, g, c: (0, cc_blk + g)),
            pl.BlockSpec((1, gw), lambda b, g, c: (0, g)),
            pl.BlockSpec((1, st), lambda b, g, c: (0, cb_blk + g)),
            pl.BlockSpec((1, st), lambda b, g, c: (0, cc_blk + g)),
            pl.BlockSpec((SUBLANE, LANE), lambda b, g, c: (g, 0)),
            pl.BlockSpec((SUBLANE, LANE), lambda b, g, c: (g, 0)),
            pl.BlockSpec((1, gw), lambda b, g, c: (0, g)),
            pl.BlockSpec((1, gw), lambda b, g, c: (0, g)),
            pl.BlockSpec((L, L), lambda b, g, c: (0, 0)),
            pl.BlockSpec((LANE, gw), lambda b, g, c: (0, 0)),
            pl.BlockSpec((taps * L, 2 * L), lambda b, g, c: (0, 0)),
        ],
        out_specs=pl.BlockSpec((L, gw), lambda b, g, c: (rows(b, g, c), g)),
        out_shape=jax.ShapeDtypeStruct((n, SSD_WIDTH), BF16),
        scratch_shapes=[
            pltpu.VMEM((st, gw), F32),
            pltpu.VMEM((2, L, gw + 2 * st), BF16),
        ],
        compiler_params=_params("parallel", "parallel", "arbitrary"),
        name="ssd",
    )(proj, proj, proj, proj, dt_raw, conv_w, conv_w, conv_w, conv_b, conv_b, conv_b,
      dtb, alog, dskip_e, onorm, tril, expand, shift)


def _sb_body(q_ref, k_ref, v_ref, uu_ref, o_ref):
    tq, kb = SB_QUERIES, SB_KEYS
    nd = tq // kb
    qi = pl.program_id(2)
    q = q_ref[...]
    uu = uu_ref[...]
    row = lax.broadcasted_iota(jnp.int32, (tq, kb), 0)
    lane = lax.broadcasted_iota(jnp.int32, (tq, kb), 1)

    def sweep(base, masked, acc, carry):
        base = pl.multiple_of(base, tq)
        z_all = lax.dot_general(q, k_ref[pl.ds(base, tq), :], (((1,), (1,)), ((), ())),
                                preferred_element_type=F32)
        parts = []
        for d in range(nd):
            z = z_all[:, d * kb:(d + 1) * kb]
            sp = jnp.log(1.0 + jnp.exp2(jnp.abs(z) * (-LOG2E)))
            log_beta = jnp.minimum(z, 0.0) - sp
            log_keep = log_beta - z
            if masked:
                mask = (d * kb + lane) < row
                log_keep = jnp.where(mask, log_keep, 0.0)
                log_beta = jnp.where(mask, log_beta, MASKED_LOG)
            hi = log_keep.astype(BF16)
            lo = (log_keep - hi.astype(F32)).astype(BF16)
            parts.append((log_beta, _dot(jnp.concatenate([hi, lo], axis=1), uu)))
        ws = [None] * nd
        for d in reversed(range(nd)):
            log_beta, sums = parts[d]
            ws[d] = jnp.exp(log_beta + sums[:, :kb] + carry).astype(BF16)
            carry = carry + sums[:, kb:]
        acc = acc + _dot(jnp.concatenate(ws, axis=1), v_ref[pl.ds(base, tq), :])
        return acc, carry

    acc = jnp.zeros((tq, SB_HEAD_DIM), F32)
    carry = jnp.zeros((tq, kb), F32)
    acc, carry = sweep(qi * tq, True, acc, carry)
    odd = qi % 2
    acc, carry = lax.cond(odd == 1, lambda a, c: sweep((qi - 1) * tq, False, a, c), lambda a, c: (a, c),
                          acc, carry)

    def pair(i, ac):
        base = (qi - odd - 1 - 2 * i) * tq
        a, c = sweep(base, False, ac[0], ac[1])
        return sweep(base - tq, False, a, c)

    acc, carry = lax.fori_loop(0, qi // 2, pair, (acc, carry))
    o_ref[...] = acc.astype(BF16)


def _sb_attention(proj, batch, seq, q_blk, k_blk, v_blk):
    n = proj.shape[0]
    tq, kb, hd = SB_QUERIES, SB_KEYS, SB_HEAD_DIM
    nq = seq // tq
    r = lax.broadcasted_iota(jnp.int32, (2 * kb, 2 * kb), 0) % kb
    c = lax.broadcasted_iota(jnp.int32, (2 * kb, 2 * kb), 1)
    uu = jnp.logical_or(c >= kb, r > c).astype(BF16)
    return pl.pallas_call(
        _sb_body,
        grid=(batch, SB_HEADS, nq),
        in_specs=[
            pl.BlockSpec((tq, hd), lambda b, h, i: (b * nq + i, q_blk + h)),
            pl.BlockSpec((seq, hd), lambda b, h, i: (b, k_blk + h)),
            pl.BlockSpec((seq, hd), lambda b, h, i: (b, v_blk + h)),
            pl.BlockSpec((2 * kb, 2 * kb), lambda b, h, i: (0, 0)),
        ],
        out_specs=pl.BlockSpec((tq, hd), lambda b, h, i: (b * nq + i, h)),
        out_shape=jax.ShapeDtypeStruct((n, SB_WIDTH), BF16),
        compiler_params=_params("parallel", "parallel", "arbitrary"),
        name="sb_attention",
    )(proj, proj, proj, uu)


def _outproj_body(ya_ref, yb_ref, wa_ref, wb_ref, x_ref, o_ref):
    o_ref[...] = x_ref[...] + _dot(ya_ref[...], wa_ref[...]) + _dot(yb_ref[...], wb_ref[...])


def _outproj(y_ssd, y_sb, w_a, w_b, xf):
    n, d = xf.shape
    tm, tn = OUTPROJ_TM, OUTPROJ_TN
    ka, kb = y_ssd.shape[1], y_sb.shape[1]
    return pl.pallas_call(
        _outproj_body,
        grid=(d // tn, n // tm),
        in_specs=[
            pl.BlockSpec((tm, ka), lambda j, i: (i, 0)),
            pl.BlockSpec((tm, kb), lambda j, i: (i, 0)),
            pl.BlockSpec((ka, tn), lambda j, i: (0, j)),
            pl.BlockSpec((kb, tn), lambda j, i: (0, j)),
            pl.BlockSpec((tm, tn), lambda j, i: (i, j)),
        ],
        out_specs=pl.BlockSpec((tm, tn), lambda j, i: (i, j)),
        out_shape=jax.ShapeDtypeStruct((n, d), F32),
        compiler_params=_params("parallel", "parallel"),
        name="outproj",
    )(y_ssd, y_sb, w_a, w_b, xf)


def _mlp_body(x_ref, g_ref, wu_ref, wd_ref, o_ref, xn_ref):
    f = pl.program_id(1)

    @pl.when(f == 0)
    def _():
        x = x_ref[...]
        xn_ref[...] = ((x * _rms_scale(x)) * g_ref[...]).astype(BF16)
        o_ref[...] = x

    u = jnp.maximum(_dot(xn_ref[...], wu_ref[...]), 0.0)
    o_ref[...] += _dot((u * u).astype(BF16), wd_ref[...])


def _mlp(xf, gain, w_up, w_down):
    n, d = xf.shape
    ff = w_up.shape[1]
    tm, tf = MLP_TM, MLP_TF
    return pl.pallas_call(
        _mlp_body,
        grid=(n // tm, ff // tf),
        in_specs=[
            pl.BlockSpec((tm, d), lambda i, f: (i, 0)),
            pl.BlockSpec((1, d), lambda i, f: (0, 0)),
            pl.BlockSpec((d, tf), lambda i, f: (0, f)),
            pl.BlockSpec((tf, d), lambda i, f: (f, 0)),
        ],
        out_specs=pl.BlockSpec((tm, d), lambda i, f: (i, 0)),
        out_shape=jax.ShapeDtypeStruct((n, d), F32),
        scratch_shapes=[pltpu.VMEM((tm, d), BF16)],
        compiler_params=_params("parallel", "arbitrary"),
        name="mlp",
    )(xf, gain, w_up, w_down)


def _pool_body(x_ref, halo_ref, g_ref, w_ref, b_ref, s_ref, o_ref, *, seq):
    tm = x_ref.shape[0]
    gch = w_ref.shape[1]
    t0 = (pl.program_id(0) * tm) % seq
    x = x_ref[...]
    gain = g_ref[...]
    h = (x * _rms_scale(x)) * gain
    halo = halo_ref[...]
    hh = (halo * _rms_scale(halo)) * gain
    hh = jnp.where(t0 == 0, 0.0, hh)
    ext = jnp.concatenate([hh, h], axis=0)
    pos = t0 + lax.broadcasted_iota(jnp.int32, (tm, 1), 0)
    for gi, win in enumerate(POOL_WINDOWS):
        sl = slice(gi * gch, (gi + 1) * gch)
        s = ext[:, sl]
        off, width = 0, 1
        while width < win:
            s = s[width:, :] + s[:-width, :]
            off += width
            width *= 2
        s = s[POOL_HALO - off:POOL_HALO - off + tm, :]
        count = jnp.minimum(pos + 1, win).astype(F32)
        dgrp = (s / count - h[:, sl]).astype(BF16)
        y = _dot(dgrp, w_ref[gi]) + b_ref[:, sl]
        o_ref[:, sl] = x[:, sl] + y * s_ref[:, sl]


def _pool(xf, gain, w, b, scale, seq):
    n, d = xf.shape
    tm = POOL_TM
    per = tm // POOL_HALO
    return pl.pallas_call(
        functools.partial(_pool_body, seq=seq),
        grid=(n // tm,),
        in_specs=[
            pl.BlockSpec((tm, d), lambda i: (i, 0)),
            pl.BlockSpec((POOL_HALO, d), lambda i: (jnp.maximum(i * per - 1, 0), 0)),
            pl.BlockSpec((1, d), lambda i: (0, 0)),
            pl.BlockSpec(w.shape, lambda i: (0, 0, 0)),
            pl.BlockSpec((1, d), lambda i: (0, 0)),
            pl.BlockSpec((1, d), lambda i: (0, 0)),
        ],
        out_specs=pl.BlockSpec((tm, d), lambda i: (i, 0)),
        out_shape=jax.ShapeDtypeStruct((n, d), F32),
        compiler_params=_params("parallel"),
        name="pool",
    )(xf, xf, gain, w, b, scale)


def _hybrid_layer(xf, batch, seq, norm, w_in, conv_w, conv_b, dt_bias, a_log, d_skip, out_norm,
                  q_norm, k_norm, w_out):
    d = xf.shape[1]
    conv_dim = conv_w.shape[1]
    dt_lo = SSD_WIDTH + conv_dim
    dt_hi = dt_lo + SSD_HEADS
    w_main = jnp.concatenate([w_in[:, :dt_lo], w_in[:, dt_hi:]], axis=1).astype(BF16)
    hpg = SSD_HEADS_PER_GROUP

    def per_group(v):
        v = v.reshape(v.shape[:-1] + (SSD_GROUPS, hpg))
        return jnp.pad(v, [(0, 0)] * (v.ndim - 1) + [(0, LANE - hpg)])

    w_dt = per_group(w_in[:, dt_lo:dt_hi]).reshape(d, SSD_GROUPS * LANE).astype(BF16)

    def group_rows(v):
        v = per_group(v.astype(F32))[:, None, :]
        return jnp.pad(v, [(0, 0), (0, SUBLANE - 1), (0, 0)]).reshape(SSD_GROUPS * SUBLANE, LANE)

    q_lo = dt_lo // INPROJ_TN
    sb_tiles = SB_WIDTH // INPROJ_TN
    proj, dt_raw = _inproj(xf, norm.reshape(1, d), w_main, w_dt,
                           q_norm.reshape(1, -1), k_norm.reshape(1, -1),
                           (q_lo, q_lo + sb_tiles), (q_lo + sb_tiles, q_lo + 2 * sb_tiles))
    y_ssd = _ssd(proj, dt_raw, conv_w, conv_b.reshape(1, -1), group_rows(dt_bias), group_rows(a_log),
                 jnp.repeat(d_skip.astype(F32), SSD_HEAD_DIM).reshape(1, -1), out_norm.reshape(1, -1),
                 batch, seq)
    q_blk = dt_lo // SB_HEAD_DIM
    y_sb = _sb_attention(proj, batch, seq, q_blk, q_blk + SB_HEADS, q_blk + 2 * SB_HEADS)
    w_out_b = w_out.astype(BF16)
    return _outproj(y_ssd, y_sb, w_out_b[:SSD_WIDTH], w_out_b[SSD_WIDTH:], xf)


def kernel(x, hyb_norm, hyb_w_in, ssd_conv_w, ssd_conv_b, ssd_dt_bias, ssd_a_log, ssd_d, ssd_out_norm, sb_q_norm, sb_k_norm, hyb_w_out, pool_norm, pool_w, pool_b, pool_scale, mlp_norm, mlp_w_up, mlp_w_down):
    batch, seq, d = x.shape
    xf = x.reshape(batch * seq, d)
    for layer in range(mlp_norm.shape[0]):
        i = layer // 2
        if layer % 2 == 0:
            xf = _hybrid_layer(xf, batch, seq, hyb_norm[i], hyb_w_in[i], ssd_conv_w[i], ssd_conv_b[i],
                               ssd_dt_bias[i], ssd_a_log[i], ssd_d[i], ssd_out_norm[i],
                               sb_q_norm[i], sb_k_norm[i], hyb_w_out[i])
        else:
            xf = _pool(xf, pool_norm[i].reshape(1, d), pool_w[i].astype(BF16), pool_b[i].reshape(1, d),
                       pool_scale[i].reshape(1, d), seq)
        xf = _mlp(xf, mlp_norm[layer].reshape(1, d), mlp_w_up[layer].astype(BF16),
                  mlp_w_down[layer].astype(BF16))
    return xf.reshape(batch, seq, d)
```

```python
import functools

import jax
import jax.numpy as jnp
from jax import lax
from jax.experimental import pallas as pl
from jax.experimental.pallas import tpu as pltpu

F32 = jnp.float32
BF16 = jnp.bfloat16
EPS = 1e-6
LOG2E = 1.4426950408889634
MASKED_LOG = -1e30

SSD_HEADS = 32
SSD_HEAD_DIM = 64
SSD_STATE = 128
SSD_GROUPS = 4
SSD_CONV = 4
SSD_HEADS_PER_GROUP = SSD_HEADS // SSD_GROUPS
SSD_GROUP_WIDTH = SSD_HEADS_PER_GROUP * SSD_HEAD_DIM
SSD_WIDTH = SSD_HEADS * SSD_HEAD_DIM
SB_HEADS = 16
SB_HEAD_DIM = 128
SB_WIDTH = SB_HEADS * SB_HEAD_DIM
POOL_WINDOWS = (2, 4, 8, 16)
POOL_HALO = 16

LANE = 128
SUBLANE = 8
VMEM_LIMIT_BYTES = 56 * 1024 * 1024

SSD_CHUNK = 128
SSD_GROUPS_PER_STEP = 2
SB_KEYS = 128
SB_QUERIES = 512
INPROJ_TM, INPROJ_TN = 1024, 1024
OUTPROJ_TM, OUTPROJ_TN = 512, 1024
MLP_TM, MLP_TF = 1024, 512
POOL_TM = 512
CAST_BLOCK_ELEMS = 2 * 1024 * 1024


def _params(*semantics):
    return pltpu.CompilerParams(dimension_semantics=semantics, vmem_limit_bytes=VMEM_LIMIT_BYTES)


def _rms_scale(v):
    return lax.rsqrt(jnp.mean(v * v, axis=-1, keepdims=True) + EPS)


def _silu(v):
    h = 0.5 * v
    return h + h * jnp.tanh(h)


def _split3(v):
    hi = v.astype(BF16)
    r = v - hi.astype(F32)
    mid = r.astype(BF16)
    lo = (r - mid.astype(F32)).astype(BF16)
    return hi, mid, lo


def _dot(a, b):
    return jnp.dot(a, b, preferred_element_type=F32)


def _cast_body(w_ref, o_ref):
    o_ref[...] = w_ref[...].astype(BF16)


def _cast_layer(w_stacked, layer):
    _, rows, cols = w_stacked.shape
    tr = max(SUBLANE, min(rows, CAST_BLOCK_ELEMS // cols))
    return pl.pallas_call(
        _cast_body,
        grid=(rows // tr,),
        in_specs=[pl.BlockSpec((None, tr, cols), lambda i: (layer, i, 0))],
        out_specs=pl.BlockSpec((tr, cols), lambda i: (i, 0)),
        out_shape=jax.ShapeDtypeStruct((rows, cols), BF16),
        compiler_params=_params("parallel"),
        name="cast_bf16",
    )(w_stacked)


def _inproj_body(x_ref, g_ref, w_ref, wdt_ref, qg_ref, kg_ref, o_ref, dt_ref, xn_ref, *,
                 q_tiles, k_tiles, sb_scale):
    j = pl.program_id(1)

    @pl.when(j == 0)
    def _():
        x = x_ref[...]
        xn_ref[...] = ((x * _rms_scale(x)) * g_ref[...]).astype(BF16)
        dt_ref[...] = _dot(xn_ref[...], wdt_ref[...])

    acc = _dot(xn_ref[...], w_ref[...])
    heads_per_tile = acc.shape[1] // SB_HEAD_DIM

    def head_norm(gain):
        for c in range(heads_per_tile):
            a = acc[:, c * SB_HEAD_DIM:(c + 1) * SB_HEAD_DIM]
            o_ref[:, c * SB_HEAD_DIM:(c + 1) * SB_HEAD_DIM] = ((a * _rms_scale(a)) * gain).astype(BF16)

    is_q = jnp.logical_and(j >= q_tiles[0], j < q_tiles[1])
    is_k = jnp.logical_and(j >= k_tiles[0], j < k_tiles[1])

    @pl.when(is_q)
    def _():
        head_norm(qg_ref[...] * sb_scale)

    @pl.when(is_k)
    def _():
        head_norm(kg_ref[...])

    @pl.when(jnp.logical_not(jnp.logical_or(is_q, is_k)))
    def _():
        o_ref[...] = acc.astype(BF16)


def _inproj(xf, gain, w_main, w_dt, q_gain, k_gain, q_tiles, k_tiles):
    n, d = xf.shape
    cols = w_main.shape[1]
    tm, tn = INPROJ_TM, INPROJ_TN
    body = functools.partial(_inproj_body, q_tiles=q_tiles, k_tiles=k_tiles,
                             sb_scale=SB_HEAD_DIM ** -0.5)
    return pl.pallas_call(
        body,
        grid=(n // tm, cols // tn),
        in_specs=[
            pl.BlockSpec((tm, d), lambda i, j: (i, 0)),
            pl.BlockSpec((1, d), lambda i, j: (0, 0)),
            pl.BlockSpec((d, tn), lambda i, j: (0, j)),
            pl.BlockSpec((d, w_dt.shape[1]), lambda i, j: (0, 0)),
            pl.BlockSpec((1, SB_HEAD_DIM), lambda i, j: (0, 0)),
            pl.BlockSpec((1, SB_HEAD_DIM), lambda i, j: (0, 0)),
        ],
        out_specs=[
            pl.BlockSpec((tm, tn), lambda i, j: (i, j)),
            pl.BlockSpec((tm, w_dt.shape[1]), lambda i, j: (i, 0)),
        ],
        out_shape=[
            jax.ShapeDtypeStruct((n, cols), BF16),
            jax.ShapeDtypeStruct((n, w_dt.shape[1]), F32),
        ],
        scratch_shapes=[pltpu.VMEM((tm, d), BF16)],
        compiler_params=_params("parallel", "arbitrary"),
        name="inproj",
    )(xf, gain, w_main, w_dt, q_gain, k_gain)


def _ssd_body(z_ref, x_ref, b_ref, c_ref, dt_ref, cwx_ref, cwb_ref, cwc_ref, cbx_ref, cbb_ref, cbc_ref,
              dtb_ref, alog_ref, dsk_ref, onorm_ref, tril_ref, e_ref, shift_ref, o_ref,
              state_ref, prev_ref):
    L = SSD_CHUNK
    gw, st = SSD_GROUP_WIDTH, SSD_STATE
    step = pl.program_id(2)

    @pl.when(step == 0)
    def _():
        state_ref[...] = jnp.zeros_like(state_ref)
        prev_ref[0] = jnp.zeros(prev_ref.shape[1:], prev_ref.dtype)

    tril = tril_ref[...]
    e = e_ref[...]
    shift = shift_ref[...]
    row = lax.broadcasted_iota(jnp.int32, (L, L), 0)
    col = lax.broadcasted_iota(jnp.int32, (L, L), 1)
    causal = row >= col
    lane = lax.broadcasted_iota(jnp.int32, (L, LANE), 1)

    for j in range(SSD_GROUPS_PER_STEP):
        xsl = slice(j * gw, (j + 1) * gw)
        ssl = slice(j * st, (j + 1) * st)

        cur = jnp.concatenate([x_ref[:, xsl], b_ref[:, ssl], c_ref[:, ssl]], axis=1)
        shifted = _dot(shift, jnp.concatenate([prev_ref[step % 2, j], cur], axis=0))
        prev_ref[(step + 1) % 2, j] = cur

        def conv_silu(lo, hi, w, bias):
            y = bias + w[SSD_CONV - 1:SSD_CONV, :] * cur[:, lo:hi].astype(F32)
            for k in range(SSD_CONV - 1):
                y = y + w[k:k + 1, :] * shifted[k * L:(k + 1) * L, lo:hi]
            return _silu(y)

        xs = conv_silu(0, gw, cwx_ref[:, xsl], cbx_ref[:, xsl])
        bm = conv_silu(gw, gw + st, cwb_ref[:, ssl], cbb_ref[:, ssl]).astype(BF16)
        cm = conv_silu(gw + st, gw + 2 * st, cwc_ref[:, ssl], cbc_ref[:, ssl]).astype(BF16)

        prow = slice(j * SUBLANE, j * SUBLANE + 1)
        pre = dt_ref[:, j * LANE:(j + 1) * LANE] + dtb_ref[prow, :]
        dt = jnp.maximum(pre, 0.0) + jnp.log(1.0 + jnp.exp(-jnp.abs(pre)))
        da = dt * (-jnp.exp(alog_ref[prow, :]))
        cs = sum(_dot(tril, p) for p in _split3(da))
        cs_t = cs.T

        both = jnp.concatenate([dt, cs], axis=0)
        both_e = sum(_dot(p, e) for p in _split3(both))
        dt_e, cs_e = both_e[:L], both_e[L:]
        cs_last = cs_e[L - 1:L, :]

        xdt = xs * dt_e
        xdt_b = xdt.astype(BF16)
        cb = lax.dot_general(cm, bm, (((1,), (1,)), ((), ())), preferred_element_type=F32)

        ys = []
        for pr in range(SSD_HEADS_PER_GROUP // 2):
            ms = []
            for h in (2 * pr, 2 * pr + 1):
                seg = cs[:, h:h + 1] - cs_t[h:h + 1, :]
                dec = jnp.exp(jnp.where(causal, seg, -jnp.inf))
                ms.append((dec * cb).astype(BF16))
            xt = xdt_b[:, pr * LANE:(pr + 1) * LANE]
            zero = jnp.zeros_like(xt)
            rhs = jnp.concatenate([jnp.where(lane < SSD_HEAD_DIM, xt, zero),
                                   jnp.where(lane >= SSD_HEAD_DIM, xt, zero)], axis=0)
            ys.append(_dot(jnp.concatenate(ms, axis=1), rhs))
        y_diag = jnp.concatenate(ys, axis=1)

        state = state_ref[j]
        y_off = _dot(cm, state.astype(BF16)) * jnp.exp(cs_e)
        xw = (xdt * jnp.exp(cs_last - cs_e)).astype(BF16)
        upd = lax.dot_general(bm, xw, (((0,), (0,)), ((), ())), preferred_element_type=F32)
        state_ref[j] = state * jnp.exp(cs_last) + upd

        y = y_diag + y_off + dsk_ref[:, xsl] * xs
        gated = y * _silu(z_ref[:, xsl].astype(F32))
        o_ref[:, xsl] = ((gated * _rms_scale(gated)) * onorm_ref[:, xsl]).astype(BF16)


def _ssd(proj, dt_raw, conv_w, conv_b, dtb, alog, dskip_e, onorm, batch, seq):
    n = proj.shape[0]
    L = SSD_CHUNK
    nc = seq // L
    gps = SSD_GROUPS_PER_STEP
    gw, st = SSD_GROUP_WIDTH, SSD_STATE
    bw, bs = gps * gw, gps * st
    x_blk = SSD_WIDTH // bw
    b_blk = 2 * SSD_WIDTH // bs
    c_blk = b_blk + SSD_GROUPS // gps
    cb_blk = SSD_WIDTH // bs
    cc_blk = cb_blk + SSD_GROUPS // gps
    tril = (lax.broadcasted_iota(jnp.int32, (L, L), 0) >= lax.broadcasted_iota(jnp.int32, (L, L), 1)).astype(BF16)
    expand = (lax.broadcasted_iota(jnp.int32, (LANE, gw), 0)
              == lax.broadcasted_iota(jnp.int32, (LANE, gw), 1) // SSD_HEAD_DIM).astype(BF16)
    taps = SSD_CONV - 1
    r = lax.broadcasted_iota(jnp.int32, (taps * L, 2 * L), 0)
    shift = (lax.broadcasted_iota(jnp.int32, (taps * L, 2 * L), 1) == L + r % L - taps + r // L).astype(BF16)
    rows = lambda b, g, c: b * nc + c
    return pl.pallas_call(
        _ssd_body,
        grid=(batch, SSD_GROUPS // gps, nc),
        in_specs=[
            pl.BlockSpec((L, bw), lambda b, g, c: (rows(b, g, c), g)),
            pl.BlockSpec((L, bw), lambda b, g, c: (rows(b, g, c), x_blk + g)),
            pl.BlockSpec((L, bs), lambda b, g, c: (rows(b, g, c), b_blk + g)),
            pl.BlockSpec((L, bs), lambda b, g, c: (rows(b, g, c), c_blk + g)),
            pl.BlockSpec((L, gps * LANE), lambda b, g, c: (rows(b, g, c), g)),
            pl.BlockSpec((SSD_CONV, bw), lambda b, g, c: (0, g)),
            pl.BlockSpec((SSD_CONV, bs), lambda b, g, c: (0, cb_blk + g)),
            pl.BlockSpec((SSD_CONV, bs), lambda b, g, c: (0, cc_blk + g)),
            pl.BlockSpec((1, bw), lambda b, g, c: (0, g)),
            pl.BlockSpec((1, bs), lambda b, g, c: (0, cb_blk + g)),
            pl.BlockSpec((1, bs), lambda b, g, c: (0, cc_blk + g)),
            pl.BlockSpec((gps * SUBLANE, LANE), lambda b, g, c: (g, 0)),
            pl.BlockSpec((gps * SUBLANE, LANE), lambda b, g, c: (g, 0)),
            pl.BlockSpec((1, bw), lambda b, g, c: (0, g)),
            pl.BlockSpec((1, bw), lambda b, g, c: (0, g)),
            pl.BlockSpec((L, L), lambda b, g, c: (0, 0)),
            pl.BlockSpec((LANE, gw), lambda b, g, c: (0, 0)),
            pl.BlockSpec((taps * L, 2 * L), lambda b, g, c: (0, 0)),
        ],
        out_specs=pl.BlockSpec((L, bw), lambda b, g, c: (rows(b, g, c), g)),
        out_shape=jax.ShapeDtypeStruct((n, SSD_WIDTH), BF16),
        scratch_shapes=[
            pltpu.VMEM((gps, st, gw), F32),
            pltpu.VMEM((2, gps, L, gw + 2 * st), BF16),
        ],
        compiler_params=_params("parallel", "parallel", "arbitrary"),
        name="ssd",
    )(proj, proj, proj, proj, dt_raw, conv_w, conv_w, conv_w, conv_b, conv_b, conv_b,
      dtb, alog, dskip_e, onorm, tril, expand, shift)


def _sb_body(q_ref, k_ref, v_ref, uu_ref, o_ref):
    tq, kb = SB_QUERIES, SB_KEYS
    nd = tq // kb
    qi = pl.program_id(2)
    q = q_ref[...]
    uu = uu_ref[...]

    def sweep(base, masked, acc, carry):
        base = pl.multiple_of(base, tq)
        z_all = lax.dot_general(q, k_ref[pl.ds(base, tq), :], (((1,), (1,)), ((), ())),
                                preferred_element_type=F32)
        parts = []
        for d in range(nd):
            r0 = d * kb if masked else 0
            z = z_all[r0:, d * kb:(d + 1) * kb]
            sp = jnp.log(1.0 + jnp.exp2(jnp.abs(z) * (-LOG2E)))
            log_beta = jnp.minimum(z, 0.0) - sp
            log_keep = log_beta - z
            if masked:
                rows_d = r0 + lax.broadcasted_iota(jnp.int32, (tq - r0, kb), 0)
                mask = (d * kb + lax.broadcasted_iota(jnp.int32, (tq - r0, kb), 1)) < rows_d
                log_keep = jnp.where(mask, log_keep, 0.0)
                log_beta = jnp.where(mask, log_beta, MASKED_LOG)
            hi = log_keep.astype(BF16)
            lo = (log_keep - hi.astype(F32)).astype(BF16)
            parts.append((r0, log_beta, _dot(jnp.concatenate([hi, lo], axis=1), uu)))
        ws = [None] * nd
        for d in reversed(range(nd)):
            r0, log_beta, sums = parts[d]
            w = jnp.exp(log_beta + sums[:, :kb] + carry[r0:]).astype(BF16)
            tot = carry[r0:] + sums[:, kb:]
            if r0:
                w = jnp.concatenate([jnp.zeros((r0, kb), BF16), w], axis=0)
                tot = jnp.concatenate([carry[:r0], tot], axis=0)
            ws[d], carry = w, tot
        acc = acc + _dot(jnp.concatenate(ws, axis=1), v_ref[pl.ds(base, tq), :])
        return acc, carry

    acc = jnp.zeros((tq, SB_HEAD_DIM), F32)
    carry = jnp.zeros((tq, kb), F32)
    acc, carry = sweep(qi * tq, True, acc, carry)
    odd = qi % 2
    acc, carry = lax.cond(odd == 1, lambda a, c: sweep((qi - 1) * tq, False, a, c), lambda a, c: (a, c),
                          acc, carry)

    def pair(i, ac):
        base = (qi - odd - 1 - 2 * i) * tq
        a, c = sweep(base, False, ac[0], ac[1])
        return sweep(base - tq, False, a, c)

    acc, carry = lax.fori_loop(0, qi // 2, pair, (acc, carry))
    o_ref[...] = acc.astype(BF16)


def _sb_attention(proj, batch, seq, q_blk, k_blk, v_blk):
    n = proj.shape[0]
    tq, kb, hd = SB_QUERIES, SB_KEYS, SB_HEAD_DIM
    nq = seq // tq
    r = lax.broadcasted_iota(jnp.int32, (2 * kb, 2 * kb), 0) % kb
    c = lax.broadcasted_iota(jnp.int32, (2 * kb, 2 * kb), 1)
    uu = jnp.logical_or(c >= kb, r > c).astype(BF16)
    return pl.pallas_call(
        _sb_body,
        grid=(batch, SB_HEADS, nq),
        in_specs=[
            pl.BlockSpec((tq, hd), lambda b, h, i: (b * nq + i, q_blk + h)),
            pl.BlockSpec((seq, hd), lambda b, h, i: (b, k_blk + h)),
            pl.BlockSpec((seq, hd), lambda b, h, i: (b, v_blk + h)),
            pl.BlockSpec((2 * kb, 2 * kb), lambda b, h, i: (0, 0)),
        ],
        out_specs=pl.BlockSpec((tq, hd), lambda b, h, i: (b * nq + i, h)),
        out_shape=jax.ShapeDtypeStruct((n, SB_WIDTH), BF16),
        compiler_params=_params("parallel", "parallel", "arbitrary"),
        name="sb_attention",
    )(proj, proj, proj, uu)


def _outproj_body(ya_ref, yb_ref, wa_ref, wb_ref, x_ref, o_ref):
    o_ref[...] = x_ref[...] + _dot(ya_ref[...], wa_ref[...]) + _dot(yb_ref[...], wb_ref[...])


def _outproj(y_ssd, y_sb, w_out, xf):
    n, d = xf.shape
    tm, tn = OUTPROJ_TM, OUTPROJ_TN
    ka, kb = y_ssd.shape[1], y_sb.shape[1]
    assert ka == kb and w_out.shape[0] == ka + kb
    return pl.pallas_call(
        _outproj_body,
        grid=(d // tn, n // tm),
        in_specs=[
            pl.BlockSpec((tm, ka), lambda j, i: (i, 0)),
            pl.BlockSpec((tm, kb), lambda j, i: (i, 0)),
            pl.BlockSpec((ka, tn), lambda j, i: (0, j)),
            pl.BlockSpec((kb, tn), lambda j, i: (1, j)),
            pl.BlockSpec((tm, tn), lambda j, i: (i, j)),
        ],
        out_specs=pl.BlockSpec((tm, tn), lambda j, i: (i, j)),
        out_shape=jax.ShapeDtypeStruct((n, d), F32),
        compiler_params=_params("parallel", "parallel"),
        name="outproj",
    )(y_ssd, y_sb, w_out, w_out, xf)


def _mlp_body(x_ref, g_ref, wu_ref, wd_ref, o_ref, xn_ref):
    f = pl.program_id(1)

    @pl.when(f == 0)
    def _():
        x = x_ref[...]
        xn_ref[...] = ((x * _rms_scale(x)) * g_ref[...]).astype(BF16)
        o_ref[...] = x

    u = jnp.maximum(_dot(xn_ref[...], wu_ref[...]), 0.0)
    o_ref[...] += _dot((u * u).astype(BF16), wd_ref[...])


def _mlp(xf, gain, w_up, w_down):
    n, d = xf.shape
    ff = w_up.shape[1]
    tm, tf = MLP_TM, MLP_TF
    return pl.pallas_call(
        _mlp_body,
        grid=(n // tm, ff // tf),
        in_specs=[
            pl.BlockSpec((tm, d), lambda i, f: (i, 0)),
            pl.BlockSpec((1, d), lambda i, f: (0, 0)),
            pl.BlockSpec((d, tf), lambda i, f: (0, f)),
            pl.BlockSpec((tf, d), lambda i, f: (f, 0)),
        ],
        out_specs=pl.BlockSpec((tm, d), lambda i, f: (i, 0)),
        out_shape=jax.ShapeDtypeStruct((n, d), F32),
        scratch_shapes=[pltpu.VMEM((tm, d), BF16)],
        compiler_params=_params("parallel", "arbitrary"),
        name="mlp",
    )(xf, gain, w_up, w_down)


def _pool_body(x_ref, halo_ref, g_ref, w_ref, b_ref, s_ref, o_ref, *, seq):
    tm = x_ref.shape[0]
    gch = w_ref.shape[1]
    t0 = (pl.program_id(0) * tm) % seq
    x = x_ref[...]
    gain = g_ref[...]
    h = (x * _rms_scale(x)) * gain
    halo = halo_ref[...]
    hh = (halo * _rms_scale(halo)) * gain
    hh = jnp.where(t0 == 0, 0.0, hh)
    ext = jnp.concatenate([hh, h], axis=0)
    pos = t0 + lax.broadcasted_iota(jnp.int32, (tm, 1), 0)
    for gi, win in enumerate(POOL_WINDOWS):
        sl = slice(gi * gch, (gi + 1) * gch)
        s = ext[:, sl]
        off, width = 0, 1
        while width < win:
            s = s[width:, :] + s[:-width, :]
            off += width
            width *= 2
        s = s[POOL_HALO - off:POOL_HALO - off + tm, :]
        count = jnp.minimum(pos + 1, win).astype(F32)
        dgrp = (s / count - h[:, sl]).astype(BF16)
        y = _dot(dgrp, w_ref[gi]) + b_ref[:, sl]
        o_ref[:, sl] = x[:, sl] + y * s_ref[:, sl]


def _pool(xf, gain, w, b, scale, seq):
    n, d = xf.shape
    tm = POOL_TM
    per = tm // POOL_HALO
    return pl.pallas_call(
        functools.partial(_pool_body, seq=seq),
        grid=(n // tm,),
        in_specs=[
            pl.BlockSpec((tm, d), lambda i: (i, 0)),
            pl.BlockSpec((POOL_HALO, d), lambda i: (jnp.maximum(i * per - 1, 0), 0)),
            pl.BlockSpec((1, d), lambda i: (0, 0)),
            pl.BlockSpec(w.shape, lambda i: (0, 0, 0)),
            pl.BlockSpec((1, d), lambda i: (0, 0)),
            pl.BlockSpec((1, d), lambda i: (0, 0)),
        ],
        out_specs=pl.BlockSpec((tm, d), lambda i: (i, 0)),
        out_shape=jax.ShapeDtypeStruct((n, d), F32),
        compiler_params=_params("parallel"),
        name="pool",
    )(xf, xf, gain, w, b, scale)


def _hybrid_layer(xf, batch, seq, norm, w_in, conv_w, conv_b, dt_bias, a_log, d_skip, out_norm,
                  q_norm, k_norm, w_out):
    d = xf.shape[1]
    conv_dim = conv_w.shape[1]
    dt_lo = SSD_WIDTH + conv_dim
    dt_hi = dt_lo + SSD_HEADS
    w_main = jnp.concatenate([w_in[:, :dt_lo], w_in[:, dt_hi:]], axis=1).astype(BF16)
    hpg = SSD_HEADS_PER_GROUP

    def per_group(v):
        v = v.reshape(v.shape[:-1] + (SSD_GROUPS, hpg))
        return jnp.pad(v, [(0, 0)] * (v.ndim - 1) + [(0, LANE - hpg)])

    w_dt = per_group(w_in[:, dt_lo:dt_hi]).reshape(d, SSD_GROUPS * LANE).astype(BF16)

    def group_rows(v):
        v = per_group(v.astype(F32))[:, None, :]
        return jnp.pad(v, [(0, 0), (0, SUBLANE - 1), (0, 0)]).reshape(SSD_GROUPS * SUBLANE, LANE)

    q_lo = dt_lo // INPROJ_TN
    sb_tiles = SB_WIDTH // INPROJ_TN
    proj, dt_raw = _inproj(xf, norm.reshape(1, d), w_main, w_dt,
                           q_norm.reshape(1, -1), k_norm.reshape(1, -1),
                           (q_lo, q_lo + sb_tiles), (q_lo + sb_tiles, q_lo + 2 * sb_tiles))
    y_ssd = _ssd(proj, dt_raw, conv_w, conv_b.reshape(1, -1), group_rows(dt_bias), group_rows(a_log),
                 jnp.repeat(d_skip.astype(F32), SSD_HEAD_DIM).reshape(1, -1), out_norm.reshape(1, -1),
                 batch, seq)
    q_blk = dt_lo // SB_HEAD_DIM
    y_sb = _sb_attention(proj, batch, seq, q_blk, q_blk + SB_HEADS, q_blk + 2 * SB_HEADS)
    return _outproj(y_ssd, y_sb, w_out, xf)


def kernel(x, hyb_norm, hyb_w_in, ssd_conv_w, ssd_conv_b, ssd_dt_bias, ssd_a_log, ssd_d, ssd_out_norm, sb_q_norm, sb_k_norm, hyb_w_out, pool_norm, pool_w, pool_b, pool_scale, mlp_norm, mlp_w_up, mlp_w_down):
    batch, seq, d = x.shape
    xf = x.reshape(batch * seq, d)
    for layer in range(mlp_norm.shape[0]):
        i = layer // 2
        if layer % 2 == 0:
            xf = _hybrid_layer(xf, batch, seq, hyb_norm[i], hyb_w_in[i], ssd_conv_w[i], ssd_conv_b[i],
                               ssd_dt_bias[i], ssd_a_log[i], ssd_d[i], ssd_out_norm[i],
                               sb_q_norm[i], sb_k_norm[i], _cast_layer(hyb_w_out, i))
        else:
            xf = _pool(xf, pool_norm[i].reshape(1, d), pool_w[i].astype(BF16), pool_b[i].reshape(1, d),
                       pool_scale[i].reshape(1, d), seq)
        xf = _mlp(xf, mlp_norm[layer].reshape(1, d), _cast_layer(mlp_w_up, layer),
                  _cast_layer(mlp_w_down, layer))
    return xf.reshape(batch, seq, d)
```

```python
import functools

import jax
import jax.numpy as jnp
from jax import lax
from jax.experimental import pallas as pl
from jax.experimental.pallas import tpu as pltpu

F32 = jnp.float32
BF16 = jnp.bfloat16
EPS = 1e-6
LOG2E = 1.4426950408889634
MASKED_LOG = -1e30

SSD_HEADS = 32
SSD_HEAD_DIM = 64
SSD_STATE = 128
SSD_GROUPS = 4
SSD_CONV = 4
SSD_HEADS_PER_GROUP = SSD_HEADS // SSD_GROUPS
SSD_GROUP_WIDTH = SSD_HEADS_PER_GROUP * SSD_HEAD_DIM
SSD_WIDTH = SSD_HEADS * SSD_HEAD_DIM
SB_HEADS = 16
SB_HEAD_DIM = 128
SB_WIDTH = SB_HEADS * SB_HEAD_DIM
POOL_WINDOWS = (2, 4, 8, 16)
POOL_HALO = 16

LANE = 128
SUBLANE = 8
VMEM_LIMIT_BYTES = 56 * 1024 * 1024

SSD_CHUNK = 128
SSD_GROUPS_PER_STEP = 4
SB_KEYS = 128
SB_QUERIES = 512
INPROJ_TM, INPROJ_TN = 1024, 1024
OUTPROJ_TM, OUTPROJ_TN = 512, 1024
MLP_TM, MLP_TF = 1024, 512
POOL_TM = 512
CAST_BLOCK_ELEMS = 2 * 1024 * 1024


def _params(*semantics):
    return pltpu.CompilerParams(dimension_semantics=semantics, vmem_limit_bytes=VMEM_LIMIT_BYTES)


def _rms_scale(v):
    return lax.rsqrt(jnp.mean(v * v, axis=-1, keepdims=True) + EPS)


def _silu(v):
    h = 0.5 * v
    return h + h * jnp.tanh(h)


def _split3(v):
    hi = v.astype(BF16)
    r = v - hi.astype(F32)
    mid = r.astype(BF16)
    lo = (r - mid.astype(F32)).astype(BF16)
    return hi, mid, lo


def _dot(a, b):
    return jnp.dot(a, b, preferred_element_type=F32)


def _cast_body(w_ref, o_ref):
    o_ref[...] = w_ref[...].astype(BF16)


def _cast_layer(w_stacked, layer):
    _, rows, cols = w_stacked.shape
    tr = max(SUBLANE, min(rows, CAST_BLOCK_ELEMS // cols))
    return pl.pallas_call(
        _cast_body,
        grid=(rows // tr,),
        in_specs=[pl.BlockSpec((None, tr, cols), lambda i: (layer, i, 0))],
        out_specs=pl.BlockSpec((tr, cols), lambda i: (i, 0)),
        out_shape=jax.ShapeDtypeStruct((rows, cols), BF16),
        compiler_params=_params("parallel"),
        name="cast_bf16",
    )(w_stacked)


def _inproj_body(x_ref, g_ref, w_ref, wdt_ref, qg_ref, kg_ref, o_ref, dt_ref, xn_ref, *,
                 q_tiles, k_tiles, sb_scale):
    j = pl.program_id(1)

    @pl.when(j == 0)
    def _():
        x = x_ref[...]
        xn_ref[...] = ((x * _rms_scale(x)) * g_ref[...]).astype(BF16)
        dt_ref[...] = _dot(xn_ref[...], wdt_ref[...])

    acc = _dot(xn_ref[...], w_ref[...])
    heads_per_tile = acc.shape[1] // SB_HEAD_DIM

    def head_norm(gain):
        for c in range(heads_per_tile):
            a = acc[:, c * SB_HEAD_DIM:(c + 1) * SB_HEAD_DIM]
            o_ref[:, c * SB_HEAD_DIM:(c + 1) * SB_HEAD_DIM] = ((a * _rms_scale(a)) * gain).astype(BF16)

    is_q = jnp.logical_and(j >= q_tiles[0], j < q_tiles[1])
    is_k = jnp.logical_and(j >= k_tiles[0], j < k_tiles[1])

    @pl.when(is_q)
    def _():
        head_norm(qg_ref[...] * sb_scale)

    @pl.when(is_k)
    def _():
        head_norm(kg_ref[...])

    @pl.when(jnp.logical_not(jnp.logical_or(is_q, is_k)))
    def _():
        o_ref[...] = acc.astype(BF16)


def _inproj(xf, gain, w_main, w_dt, q_gain, k_gain, q_tiles, k_tiles):
    n, d = xf.shape
    cols = w_main.shape[1]
    tm, tn = INPROJ_TM, INPROJ_TN
    body = functools.partial(_inproj_body, q_tiles=q_tiles, k_tiles=k_tiles,
                             sb_scale=SB_HEAD_DIM ** -0.5)
    return pl.pallas_call(
        body,
        grid=(n // tm, cols // tn),
        in_specs=[
            pl.BlockSpec((tm, d), lambda i, j: (i, 0)),
            pl.BlockSpec((1, d), lambda i, j: (0, 0)),
            pl.BlockSpec((d, tn), lambda i, j: (0, j)),
            pl.BlockSpec((d, w_dt.shape[1]), lambda i, j: (0, 0)),
            pl.BlockSpec((1, SB_HEAD_DIM), lambda i, j: (0, 0)),
            pl.BlockSpec((1, SB_HEAD_DIM), lambda i, j: (0, 0)),
        ],
        out_specs=[
            pl.BlockSpec((tm, tn), lambda i, j: (i, j)),
            pl.BlockSpec((tm, w_dt.shape[1]), lambda i, j: (i, 0)),
        ],
        out_shape=[
            jax.ShapeDtypeStruct((n, cols), BF16),
            jax.ShapeDtypeStruct((n, w_dt.shape[1]), F32),
        ],
        scratch_shapes=[pltpu.VMEM((tm, d), BF16)],
        compiler_params=_params("parallel", "arbitrary"),
        name="inproj",
    )(xf, gain, w_main, w_dt, q_gain, k_gain)


def _ssd_body(z_ref, x_ref, b_ref, c_ref, dt_ref, cwx_ref, cwb_ref, cwc_ref, cbx_ref, cbb_ref, cbc_ref,
              dtb_ref, alog_ref, dsk_ref, onorm_ref, tril_ref, e_ref, shift_ref, o_ref,
              state_ref, prev_ref):
    L = SSD_CHUNK
    gw, st = SSD_GROUP_WIDTH, SSD_STATE
    step = pl.program_id(2)

    @pl.when(step == 0)
    def _():
        state_ref[...] = jnp.zeros_like(state_ref)
        prev_ref[0] = jnp.zeros(prev_ref.shape[1:], prev_ref.dtype)

    tril = tril_ref[...]
    e = e_ref[...]
    shift = shift_ref[...]
    row = lax.broadcasted_iota(jnp.int32, (L, L), 0)
    col = lax.broadcasted_iota(jnp.int32, (L, L), 1)
    causal = row >= col
    lane = lax.broadcasted_iota(jnp.int32, (L, LANE), 1)

    for j in range(SSD_GROUPS_PER_STEP):
        xsl = slice(j * gw, (j + 1) * gw)
        ssl = slice(j * st, (j + 1) * st)

        cur = jnp.concatenate([x_ref[:, xsl], b_ref[:, ssl], c_ref[:, ssl]], axis=1)
        shifted = _dot(shift, jnp.concatenate([prev_ref[step % 2, j], cur], axis=0))
        prev_ref[(step + 1) % 2, j] = cur

        def conv_silu(lo, hi, w, bias):
            y = bias + w[SSD_CONV - 1:SSD_CONV, :] * cur[:, lo:hi].astype(F32)
            for k in range(SSD_CONV - 1):
                y = y + w[k:k + 1, :] * shifted[k * L:(k + 1) * L, lo:hi]
            return _silu(y)

        xs = conv_silu(0, gw, cwx_ref[:, xsl], cbx_ref[:, xsl])
        bm = conv_silu(gw, gw + st, cwb_ref[:, ssl], cbb_ref[:, ssl]).astype(BF16)
        cm = conv_silu(gw + st, gw + 2 * st, cwc_ref[:, ssl], cbc_ref[:, ssl]).astype(BF16)

        prow = slice(j * SUBLANE, j * SUBLANE + 1)
        pre = dt_ref[:, j * LANE:(j + 1) * LANE] + dtb_ref[prow, :]
        dt = jnp.maximum(pre, 0.0) + jnp.log(1.0 + jnp.exp(-jnp.abs(pre)))
        da = dt * (-jnp.exp(alog_ref[prow, :]))
        cs = sum(_dot(tril, p) for p in _split3(da))
        cs_t = cs.T

        both = jnp.concatenate([dt, cs], axis=0)
        both_e = sum(_dot(p, e) for p in _split3(both))
        dt_e, cs_e = both_e[:L], both_e[L:]
        cs_last = cs_e[L - 1:L, :]

        xdt = xs * dt_e
        xdt_b = xdt.astype(BF16)
        cb = lax.dot_general(cm, bm, (((1,), (1,)), ((), ())), preferred_element_type=F32)

        ys = []
        for pr in range(SSD_HEADS_PER_GROUP // 2):
            ms = []
            for h in (2 * pr, 2 * pr + 1):
                seg = cs[:, h:h + 1] - cs_t[h:h + 1, :]
                dec = jnp.exp(jnp.where(causal, seg, -jnp.inf))
                ms.append((dec * cb).astype(BF16))
            xt = xdt_b[:, pr * LANE:(pr + 1) * LANE]
            zero = jnp.zeros_like(xt)
            rhs = jnp.concatenate([jnp.where(lane < SSD_HEAD_DIM, xt, zero),
                                   jnp.where(lane >= SSD_HEAD_DIM, xt, zero)], axis=0)
            ys.append(_dot(jnp.concatenate(ms, axis=1), rhs))
        y_diag = jnp.concatenate(ys, axis=1)

        state = state_ref[j]
        y_off = _dot(cm, state.astype(BF16)) * jnp.exp(cs_e)
        xw = (xdt * jnp.exp(cs_last - cs_e)).astype(BF16)
        upd = lax.dot_general(bm, xw, (((0,), (0,)), ((), ())), preferred_element_type=F32)
        state_ref[j] = state * jnp.exp(cs_last) + upd

        y = y_diag + y_off + dsk_ref[:, xsl] * xs
        gated = y * _silu(z_ref[:, xsl].astype(F32))
        o_ref[:, xsl] = ((gated * _rms_scale(gated)) * onorm_ref[:, xsl]).astype(BF16)


def _ssd(proj, dt_raw, conv_w, conv_b, dtb, alog, dskip_e, onorm, batch, seq):
    n = proj.shape[0]
    L = SSD_CHUNK
    nc = seq // L
    gps = SSD_GROUPS_PER_STEP
    gw, st = SSD_GROUP_WIDTH, SSD_STATE
    bw, bs = gps * gw, gps * st
    x_blk = SSD_WIDTH // bw
    b_blk = 2 * SSD_WIDTH // bs
    c_blk = b_blk + SSD_GROUPS // gps
    cb_blk = SSD_WIDTH // bs
    cc_blk = cb_blk + SSD_GROUPS // gps
    tril = (lax.broadcasted_iota(jnp.int32, (L, L), 0) >= lax.broadcasted_iota(jnp.int32, (L, L), 1)).astype(BF16)
    expand = (lax.broadcasted_iota(jnp.int32, (LANE, gw), 0)
              == lax.broadcasted_iota(jnp.int32, (LANE, gw), 1) // SSD_HEAD_DIM).astype(BF16)
    taps = SSD_CONV - 1
    r = lax.broadcasted_iota(jnp.int32, (taps * L, 2 * L), 0)
    shift = (lax.broadcasted_iota(jnp.int32, (taps * L, 2 * L), 1) == L + r % L - taps + r // L).astype(BF16)
    rows = lambda b, g, c: b * nc + c
    return pl.pallas_call(
        _ssd_body,
        grid=(batch, SSD_GROUPS // gps, nc),
        in_specs=[
            pl.BlockSpec((L, bw), lambda b, g, c: (rows(b, g, c), g)),
            pl.BlockSpec((L, bw), lambda b, g, c: (rows(b, g, c), x_blk + g)),
            pl.BlockSpec((L, bs), lambda b, g, c: (rows(b, g, c), b_blk + g)),
            pl.BlockSpec((L, bs), lambda b, g, c: (rows(b, g, c), c_blk + g)),
            pl.BlockSpec((L, gps * LANE), lambda b, g, c: (rows(b, g, c), g)),
            pl.BlockSpec((SSD_CONV, bw), lambda b, g, c: (0, g)),
            pl.BlockSpec((SSD_CONV, bs), lambda b, g, c: (0, cb_blk + g)),
            pl.BlockSpec((SSD_CONV, bs), lambda b, g, c: (0, cc_blk + g)),
            pl.BlockSpec((1, bw), lambda b, g, c: (0, g)),
            pl.BlockSpec((1, bs), lambda b, g, c: (0, cb_blk + g)),
            pl.BlockSpec((1, bs), lambda b, g, c: (0, cc_blk + g)),
            pl.BlockSpec((gps * SUBLANE, LANE), lambda b, g, c: (g, 0)),
            pl.BlockSpec((gps * SUBLANE, LANE), lambda b, g, c: (g, 0)),
            pl.BlockSpec((1, bw), lambda b, g, c: (0, g)),
            pl.BlockSpec((1, bw), lambda b, g, c: (0, g)),
            pl.BlockSpec((L, L), lambda b, g, c: (0, 0)),
            pl.BlockSpec((LANE, gw), lambda b, g, c: (0, 0)),
            pl.BlockSpec((taps * L, 2 * L), lambda b, g, c: (0, 0)),
        ],
        out_specs=pl.BlockSpec((L, bw), lambda b, g, c: (rows(b, g, c), g)),
        out_shape=jax.ShapeDtypeStruct((n, SSD_WIDTH), BF16),
        scratch_shapes=[
            pltpu.VMEM((gps, st, gw), F32),
            pltpu.VMEM((2, gps, L, gw + 2 * st), BF16),
        ],
        compiler_params=_params("parallel", "parallel", "arbitrary"),
        name="ssd",
    )(proj, proj, proj, proj, dt_raw, conv_w, conv_w, conv_w, conv_b, conv_b, conv_b,
      dtb, alog, dskip_e, onorm, tril, expand, shift)


def _sb_body(q_ref, k_ref, v_ref, uu_ref, o_ref):
    tq, kb = SB_QUERIES, SB_KEYS
    nd = tq // kb
    qi = pl.program_id(2)
    q = q_ref[...]
    uu = uu_ref[...]

    def sweep(base, masked, acc, carry):
        base = pl.multiple_of(base, tq)
        z_all = lax.dot_general(q, k_ref[pl.ds(base, tq), :], (((1,), (1,)), ((), ())),
                                preferred_element_type=F32)
        parts = []
        for d in range(nd):
            r0 = d * kb if masked else 0
            z = z_all[r0:, d * kb:(d + 1) * kb].astype(BF16)
            sp = jnp.log(1.0 + jnp.exp2(jnp.abs(z) * (-LOG2E)))
            log_beta = jnp.minimum(z, 0.0) - sp
            log_keep = jnp.minimum(-z, 0.0) - sp
            if masked:
                rows_d = r0 + lax.broadcasted_iota(jnp.int32, (tq - r0, kb), 0)
                mask = (d * kb + lax.broadcasted_iota(jnp.int32, (tq - r0, kb), 1)) < rows_d
                log_keep = jnp.where(mask, log_keep, jnp.zeros_like(log_keep))
                log_beta = jnp.where(mask, log_beta, jnp.full_like(log_beta, MASKED_LOG))
            parts.append((r0, log_beta, _dot(log_keep, uu)))
        ws = [None] * nd
        for d in reversed(range(nd)):
            r0, log_beta, sums = parts[d]
            w = jnp.exp(log_beta.astype(F32) + sums[:, :kb] + carry[r0:]).astype(BF16)
            tot = carry[r0:] + sums[:, kb:]
            if r0:
                w = jnp.concatenate([jnp.zeros((r0, kb), BF16), w], axis=0)
                tot = jnp.concatenate([carry[:r0], tot], axis=0)
            ws[d], carry = w, tot
        acc = acc + _dot(jnp.concatenate(ws, axis=1), v_ref[pl.ds(base, tq), :])
        return acc, carry

    acc = jnp.zeros((tq, SB_HEAD_DIM), F32)
    carry = jnp.zeros((tq, kb), F32)
    acc, carry = sweep(qi * tq, True, acc, carry)
    odd = qi % 2
    acc, carry = lax.cond(odd == 1, lambda a, c: sweep((qi - 1) * tq, False, a, c), lambda a, c: (a, c),
                          acc, carry)

    def pair(i, ac):
        base = (qi - odd - 1 - 2 * i) * tq
        a, c = sweep(base, False, ac[0], ac[1])
        return sweep(base - tq, False, a, c)

    acc, carry = lax.fori_loop(0, qi // 2, pair, (acc, carry))
    o_ref[...] = acc.astype(BF16)


def _sb_attention(proj, batch, seq, q_blk, k_blk, v_blk):
    n = proj.shape[0]
    tq, kb, hd = SB_QUERIES, SB_KEYS, SB_HEAD_DIM
    nq = seq // tq
    r = lax.broadcasted_iota(jnp.int32, (kb, 2 * kb), 0)
    c = lax.broadcasted_iota(jnp.int32, (kb, 2 * kb), 1)
    uu = jnp.logical_or(c >= kb, r > c).astype(BF16)
    return pl.pallas_call(
        _sb_body,
        grid=(batch, SB_HEADS, nq),
        in_specs=[
            pl.BlockSpec((tq, hd), lambda b, h, i: (b * nq + i, q_blk + h)),
            pl.BlockSpec((seq, hd), lambda b, h, i: (b, k_blk + h)),
            pl.BlockSpec((seq, hd), lambda b, h, i: (b, v_blk + h)),
            pl.BlockSpec((kb, 2 * kb), lambda b, h, i: (0, 0)),
        ],
        out_specs=pl.BlockSpec((tq, hd), lambda b, h, i: (b * nq + i, h)),
        out_shape=jax.ShapeDtypeStruct((n, SB_WIDTH), BF16),
        compiler_params=_params("parallel", "parallel", "arbitrary"),
        name="sb_attention",
    )(proj, proj, proj, uu)


def _outproj_body(ya_ref, yb_ref, wa_ref, wb_ref, x_ref, o_ref):
    o_ref[...] = x_ref[...] + _dot(ya_ref[...], wa_ref[...]) + _dot(yb_ref[...], wb_ref[...])


def _outproj(y_ssd, y_sb, w_out, xf):
    n, d = xf.shape
    tm, tn = OUTPROJ_TM, OUTPROJ_TN
    ka, kb = y_ssd.shape[1], y_sb.shape[1]
    assert ka == kb and w_out.shape[0] == ka + kb
    return pl.pallas_call(
        _outproj_body,
        grid=(d // tn, n // tm),
        in_specs=[
            pl.BlockSpec((tm, ka), lambda j, i: (i, 0)),
            pl.BlockSpec((tm, kb), lambda j, i: (i, 0)),
            pl.BlockSpec((ka, tn), lambda j, i: (0, j)),
            pl.BlockSpec((kb, tn), lambda j, i: (1, j)),
            pl.BlockSpec((tm, tn), lambda j, i: (i, j)),
        ],
        out_specs=pl.BlockSpec((tm, tn), lambda j, i: (i, j)),
        out_shape=jax.ShapeDtypeStruct((n, d), F32),
        compiler_params=_params("parallel", "parallel"),
        name="outproj",
    )(y_ssd, y_sb, w_out, w_out, xf)


def _mlp_body(x_ref, g_ref, wu_ref, wd_ref, o_ref, xn_ref):
    f = pl.program_id(1)

    @pl.when(f == 0)
    def _():
        x = x_ref[...]
        xn_ref[...] = ((x * _rms_scale(x)) * g_ref[...]).astype(BF16)
        o_ref[...] = x

    u = jnp.maximum(_dot(xn_ref[...], wu_ref[...]), 0.0)
    o_ref[...] += _dot((u * u).astype(BF16), wd_ref[...])


def _mlp(xf, gain, w_up, w_down):
    n, d = xf.shape
    ff = w_up.shape[1]
    tm, tf = MLP_TM, MLP_TF
    return pl.pallas_call(
        _mlp_body,
        grid=(n // tm, ff // tf),
        in_specs=[
            pl.BlockSpec((tm, d), lambda i, f: (i, 0)),
            pl.BlockSpec((1, d), lambda i, f: (0, 0)),
            pl.BlockSpec((d, tf), lambda i, f: (0, f)),
            pl.BlockSpec((tf, d), lambda i, f: (f, 0)),
        ],
        out_specs=pl.BlockSpec((tm, d), lambda i, f: (i, 0)),
        out_shape=jax.ShapeDtypeStruct((n, d), F32),
        scratch_shapes=[pltpu.VMEM((tm, d), BF16)],
        compiler_params=_params("parallel", "arbitrary"),
        name="mlp",
    )(xf, gain, w_up, w_down)


def _pool_body(x_ref, halo_ref, g_ref, w_ref, b_ref, s_ref, o_ref, *, seq):
    tm = x_ref.shape[0]
    gch = w_ref.shape[1]
    t0 = (pl.program_id(0) * tm) % seq
    x = x_ref[...]
    gain = g_ref[...]
    h = (x * _rms_scale(x)) * gain
    halo = halo_ref[...]
    hh = (halo * _rms_scale(halo)) * gain
    hh = jnp.where(t0 == 0, 0.0, hh)
    ext = jnp.concatenate([hh, h], axis=0)
    pos = t0 + lax.broadcasted_iota(jnp.int32, (tm, 1), 0)
    for gi, win in enumerate(POOL_WINDOWS):
        sl = slice(gi * gch, (gi + 1) * gch)
        s = ext[:, sl]
        off, width = 0, 1
        while width < win:
            s = s[width:, :] + s[:-width, :]
            off += width
            width *= 2
        s = s[POOL_HALO - off:POOL_HALO - off + tm, :]
        count = jnp.minimum(pos + 1, win).astype(F32)
        dgrp = (s / count - h[:, sl]).astype(BF16)
        y = _dot(dgrp, w_ref[gi]) + b_ref[:, sl]
        o_ref[:, sl] = x[:, sl] + y * s_ref[:, sl]


def _pool(xf, gain, w, b, scale, seq):
    n, d = xf.shape
    tm = POOL_TM
    per = tm // POOL_HALO
    return pl.pallas_call(
        functools.partial(_pool_body, seq=seq),
        grid=(n // tm,),
        in_specs=[
            pl.BlockSpec((tm, d), lambda i: (i, 0)),
            pl.BlockSpec((POOL_HALO, d), lambda i: (jnp.maximum(i * per - 1, 0), 0)),
            pl.BlockSpec((1, d), lambda i: (0, 0)),
            pl.BlockSpec(w.shape, lambda i: (0, 0, 0)),
            pl.BlockSpec((1, d), lambda i: (0, 0)),
            pl.BlockSpec((1, d), lambda i: (0, 0)),
        ],
        out_specs=pl.BlockSpec((tm, d), lambda i: (i, 0)),
        out_shape=jax.ShapeDtypeStruct((n, d), F32),
        compiler_params=_params("parallel"),
        name="pool",
    )(xf, xf, gain, w, b, scale)


def _hybrid_layer(xf, batch, seq, norm, w_in, conv_w, conv_b, dt_bias, a_log, d_skip, out_norm,
                  q_norm, k_norm, w_out):
    d = xf.shape[1]
    conv_dim = conv_w.shape[1]
    dt_lo = SSD_WIDTH + conv_dim
    dt_hi = dt_lo + SSD_HEADS
    w_main = jnp.concatenate([w_in[:, :dt_lo], w_in[:, dt_hi:]], axis=1).astype(BF16)
    hpg = SSD_HEADS_PER_GROUP

    def per_group(v):
        v = v.reshape(v.shape[:-1] + (SSD_GROUPS, hpg))
        return jnp.pad(v, [(0, 0)] * (v.ndim - 1) + [(0, LANE - hpg)])

    w_dt = per_group(w_in[:, dt_lo:dt_hi]).reshape(d, SSD_GROUPS * LANE).astype(BF16)

    def group_rows(v):
        v = per_group(v.astype(F32))[:, None, :]
        return jnp.pad(v, [(0, 0), (0, SUBLANE - 1), (0, 0)]).reshape(SSD_GROUPS * SUBLANE, LANE)

    q_lo = dt_lo // INPROJ_TN
    sb_tiles = SB_WIDTH // INPROJ_TN
    proj, dt_raw = _inproj(xf, norm.reshape(1, d), w_main, w_dt,
                           q_norm.reshape(1, -1), k_norm.reshape(1, -1),
                           (q_lo, q_lo + sb_tiles), (q_lo + sb_tiles, q_lo + 2 * sb_tiles))
    y_ssd = _ssd(proj, dt_raw, conv_w, conv_b.reshape(1, -1), group_rows(dt_bias), group_rows(a_log),
                 jnp.repeat(d_skip.astype(F32), SSD_HEAD_DIM).reshape(1, -1), out_norm.reshape(1, -1),
                 batch, seq)
    q_blk = dt_lo // SB_HEAD_DIM
    y_sb = _sb_attention(proj, batch, seq, q_blk, q_blk + SB_HEADS, q_blk + 2 * SB_HEADS)
    return _outproj(y_ssd, y_sb, w_out, xf)


def kernel(x, hyb_norm, hyb_w_in, ssd_conv_w, ssd_conv_b, ssd_dt_bias, ssd_a_log, ssd_d, ssd_out_norm, sb_q_norm, sb_k_norm, hyb_w_out, pool_norm, pool_w, pool_b, pool_scale, mlp_norm, mlp_w_up, mlp_w_down):
    batch, seq, d = x.shape
    xf = x.reshape(batch * seq, d)
    for layer in range(mlp_norm.shape[0]):
        i = layer // 2
        if layer % 2 == 0:
            xf = _hybrid_layer(xf, batch, seq, hyb_norm[i], hyb_w_in[i], ssd_conv_w[i], ssd_conv_b[i],
                               ssd_dt_bias[i], ssd_a_log[i], ssd_d[i], ssd_out_norm[i],
                               sb_q_norm[i], sb_k_norm[i], _cast_layer(hyb_w_out, i))
        else:
            xf = _pool(xf, pool_norm[i].reshape(1, d), pool_w[i].astype(BF16), pool_b[i].reshape(1, d),
                       pool_scale[i].reshape(1, d), seq)
        xf = _mlp(xf, mlp_norm[layer].reshape(1, d), _cast_layer(mlp_w_up, layer),
                  _cast_layer(mlp_w_down, layer))
    return xf.reshape(batch, seq, d)
```

```python
import functools

import jax
import jax.numpy as jnp
from jax import lax
from jax.experimental import pallas as pl
from jax.experimental.pallas import tpu as pltpu

F32 = jnp.float32
BF16 = jnp.bfloat16
EPS = 1e-6
LOG2E = 1.4426950408889634
MASKED_LOG = -1e30

SSD_HEADS = 32
SSD_HEAD_DIM = 64
SSD_STATE = 128
SSD_GROUPS = 4
SSD_CONV = 4
SSD_HEADS_PER_GROUP = SSD_HEADS // SSD_GROUPS
SSD_GROUP_WIDTH = SSD_HEADS_PER_GROUP * SSD_HEAD_DIM
SSD_WIDTH = SSD_HEADS * SSD_HEAD_DIM
SB_HEADS = 16
SB_HEAD_DIM = 128
SB_WIDTH = SB_HEADS * SB_HEAD_DIM
POOL_WINDOWS = (2, 4, 8, 16)
POOL_HALO = 16

LANE = 128
SUBLANE = 8
VMEM_LIMIT_BYTES = 56 * 1024 * 1024

SSD_CHUNK = 128
SSD_GROUPS_PER_STEP = 4
SB_KEYS = 128
SB_QUERIES = 512
INPROJ_TM, INPROJ_TN = 1024, 1024
OUTPROJ_TM, OUTPROJ_TN = 512, 1024
MLP_TM, MLP_TF = 1024, 512
POOL_TM = 512
CAST_BLOCK_ELEMS = 2 * 1024 * 1024


def _params(*semantics):
    return pltpu.CompilerParams(dimension_semantics=semantics, vmem_limit_bytes=VMEM_LIMIT_BYTES)


def _rms_scale(v):
    return lax.rsqrt(jnp.mean(v * v, axis=-1, keepdims=True) + EPS)


def _silu(v):
    h = 0.5 * v
    return h + h * jnp.tanh(h)


def _split3(v):
    hi = v.astype(BF16)
    r = v - hi.astype(F32)
    mid = r.astype(BF16)
    lo = (r - mid.astype(F32)).astype(BF16)
    return hi, mid, lo


def _dot(a, b):
    return jnp.dot(a, b, preferred_element_type=F32)


def _cast_body(w_ref, o_ref):
    o_ref[...] = w_ref[...].astype(BF16)


def _cast_layer(w_stacked, layer):
    _, rows, cols = w_stacked.shape
    tr = max(SUBLANE, min(rows, CAST_BLOCK_ELEMS // cols))
    return pl.pallas_call(
        _cast_body,
        grid=(rows // tr,),
        in_specs=[pl.BlockSpec((None, tr, cols), lambda i: (layer, i, 0))],
        out_specs=pl.BlockSpec((tr, cols), lambda i: (i, 0)),
        out_shape=jax.ShapeDtypeStruct((rows, cols), BF16),
        compiler_params=_params("parallel"),
        name="cast_bf16",
    )(w_stacked)


def _inproj_body(x_ref, g_ref, w_ref, wdt_ref, qg_ref, kg_ref, o_ref, dt_ref, xn_ref, *,
                 q_tiles, k_tiles, sb_scale):
    j = pl.program_id(1)

    @pl.when(j == 0)
    def _():
        x = x_ref[...]
        xn_ref[...] = ((x * _rms_scale(x)) * g_ref[...]).astype(BF16)
        dt_ref[...] = _dot(xn_ref[...], wdt_ref[...])

    acc = _dot(xn_ref[...], w_ref[...])
    heads_per_tile = acc.shape[1] // SB_HEAD_DIM

    def head_norm(gain):
        for c in range(heads_per_tile):
            a = acc[:, c * SB_HEAD_DIM:(c + 1) * SB_HEAD_DIM]
            o_ref[:, c * SB_HEAD_DIM:(c + 1) * SB_HEAD_DIM] = ((a * _rms_scale(a)) * gain).astype(BF16)

    is_q = jnp.logical_and(j >= q_tiles[0], j < q_tiles[1])
    is_k = jnp.logical_and(j >= k_tiles[0], j < k_tiles[1])

    @pl.when(is_q)
    def _():
        head_norm(qg_ref[...] * sb_scale)

    @pl.when(is_k)
    def _():
        head_norm(kg_ref[...])

    @pl.when(jnp.logical_not(jnp.logical_or(is_q, is_k)))
    def _():
        o_ref[...] = acc.astype(BF16)


def _inproj(xf, gain, w_main, w_dt, q_gain, k_gain, q_tiles, k_tiles):
    n, d = xf.shape
    cols = w_main.shape[1]
    tm, tn = INPROJ_TM, INPROJ_TN
    body = functools.partial(_inproj_body, q_tiles=q_tiles, k_tiles=k_tiles,
                             sb_scale=SB_HEAD_DIM ** -0.5)
    return pl.pallas_call(
        body,
        grid=(n // tm, cols // tn),
        in_specs=[
            pl.BlockSpec((tm, d), lambda i, j: (i, 0)),
            pl.BlockSpec((1, d), lambda i, j: (0, 0)),
            pl.BlockSpec((d, tn), lambda i, j: (0, j)),
            pl.BlockSpec((d, w_dt.shape[1]), lambda i, j: (0, 0)),
            pl.BlockSpec((1, SB_HEAD_DIM), lambda i, j: (0, 0)),
            pl.BlockSpec((1, SB_HEAD_DIM), lambda i, j: (0, 0)),
        ],
        out_specs=[
            pl.BlockSpec((tm, tn), lambda i, j: (i, j)),
            pl.BlockSpec((tm, w_dt.shape[1]), lambda i, j: (i, 0)),
        ],
        out_shape=[
            jax.ShapeDtypeStruct((n, cols), BF16),
            jax.ShapeDtypeStruct((n, w_dt.shape[1]), F32),
        ],
        scratch_shapes=[pltpu.VMEM((tm, d), BF16)],
        compiler_params=_params("parallel", "arbitrary"),
        name="inproj",
    )(xf, gain, w_main, w_dt, q_gain, k_gain)


def _ssd_body(z_ref, x_ref, b_ref, c_ref, dt_ref, cwx_ref, cwb_ref, cwc_ref, cbx_ref, cbb_ref, cbc_ref,
              dtb_ref, alog_ref, dsk_ref, onorm_ref, tril_ref, e_ref, shift_ref, o_ref,
              state_ref, prev_ref):
    L = SSD_CHUNK
    gw, st = SSD_GROUP_WIDTH, SSD_STATE
    step = pl.program_id(2)

    @pl.when(step == 0)
    def _():
        state_ref[...] = jnp.zeros_like(state_ref)
        prev_ref[0] = jnp.zeros(prev_ref.shape[1:], prev_ref.dtype)

    tril = tril_ref[...]
    e = e_ref[...]
    shift = shift_ref[...]
    row = lax.broadcasted_iota(jnp.int32, (L, L), 0)
    col = lax.broadcasted_iota(jnp.int32, (L, L), 1)
    causal = row >= col
    lane = lax.broadcasted_iota(jnp.int32, (L, LANE), 1)

    for j in range(SSD_GROUPS_PER_STEP):
        xsl = slice(j * gw, (j + 1) * gw)
        ssl = slice(j * st, (j + 1) * st)

        cur = jnp.concatenate([x_ref[:, xsl], b_ref[:, ssl], c_ref[:, ssl]], axis=1)
        shifted = _dot(shift, jnp.concatenate([prev_ref[step % 2, j], cur], axis=0))
        prev_ref[(step + 1) % 2, j] = cur

        def conv_silu(lo, hi, w, bias):
            y = bias + w[SSD_CONV - 1:SSD_CONV, :] * cur[:, lo:hi].astype(F32)
            for k in range(SSD_CONV - 1):
                y = y + w[k:k + 1, :] * shifted[k * L:(k + 1) * L, lo:hi]
            return _silu(y)

        xs = conv_silu(0, gw, cwx_ref[:, xsl], cbx_ref[:, xsl])
        bm = conv_silu(gw, gw + st, cwb_ref[:, ssl], cbb_ref[:, ssl]).astype(BF16)
        cm = conv_silu(gw + st, gw + 2 * st, cwc_ref[:, ssl], cbc_ref[:, ssl]).astype(BF16)

        prow = slice(j * SUBLANE, j * SUBLANE + 1)
        pre = dt_ref[:, j * LANE:(j + 1) * LANE] + dtb_ref[prow, :]
        dt = jnp.maximum(pre, 0.0) + jnp.log(1.0 + jnp.exp(-jnp.abs(pre)))
        da = dt * (-jnp.exp(alog_ref[prow, :]))
        cs = sum(_dot(tril, p) for p in _split3(da))
        cs_t = cs.T

        both = jnp.concatenate([dt, cs], axis=0)
        both_e = sum(_dot(p, e) for p in _split3(both))
        dt_e, cs_e = both_e[:L], both_e[L:]
        cs_last = cs_e[L - 1:L, :]

        xdt = xs * dt_e
        xdt_b = xdt.astype(BF16)
        cb = lax.dot_general(cm, bm, (((1,), (1,)), ((), ())), preferred_element_type=F32)

        ys = []
        for pr in range(SSD_HEADS_PER_GROUP // 2):
            ms = []
            for h in (2 * pr, 2 * pr + 1):
                seg = cs[:, h:h + 1] - cs_t[h:h + 1, :]
                dec = jnp.exp(jnp.where(causal, seg, -jnp.inf))
                ms.append((dec * cb).astype(BF16))
            xt = xdt_b[:, pr * LANE:(pr + 1) * LANE]
            zero = jnp.zeros_like(xt)
            rhs = jnp.concatenate([jnp.where(lane < SSD_HEAD_DIM, xt, zero),
                                   jnp.where(lane >= SSD_HEAD_DIM, xt, zero)], axis=0)
            ys.append(_dot(jnp.concatenate(ms, axis=1), rhs))
        y_diag = jnp.concatenate(ys, axis=1)

        state = state_ref[j]
        y_off = _dot(cm, state.astype(BF16)) * jnp.exp(cs_e)
        xw = (xdt * jnp.exp(cs_last - cs_e)).astype(BF16)
        upd = lax.dot_general(bm, xw, (((0,), (0,)), ((), ())), preferred_element_type=F32)
        state_ref[j] = state * jnp.exp(cs_last) + upd

        y = y_diag + y_off + dsk_ref[:, xsl] * xs
        gated = y * _silu(z_ref[:, xsl].astype(F32))
        o_ref[:, xsl] = ((gated * _rms_scale(gated)) * onorm_ref[:, xsl]).astype(BF16)


def _ssd(proj, dt_raw, conv_w, conv_b, dtb, alog, dskip_e, onorm, batch, seq):
    n = proj.shape[0]
    L = SSD_CHUNK
    nc = seq // L
    gps = SSD_GROUPS_PER_STEP
    gw, st = SSD_GROUP_WIDTH, SSD_STATE
    bw, bs = gps * gw, gps * st
    x_blk = SSD_WIDTH // bw
    b_blk = 2 * SSD_WIDTH // bs
    c_blk = b_blk + SSD_GROUPS // gps
    cb_blk = SSD_WIDTH // bs
    cc_blk = cb_blk + SSD_GROUPS // gps
    tril = (lax.broadcasted_iota(jnp.int32, (L, L), 0) >= lax.broadcasted_iota(jnp.int32, (L, L), 1)).astype(BF16)
    expand = (lax.broadcasted_iota(jnp.int32, (LANE, gw), 0)
              == lax.broadcasted_iota(jnp.int32, (LANE, gw), 1) // SSD_HEAD_DIM).astype(BF16)
    taps = SSD_CONV - 1
    r = lax.broadcasted_iota(jnp.int32, (taps * L, 2 * L), 0)
    shift = (lax.broadcasted_iota(jnp.int32, (taps * L, 2 * L), 1) == L + r % L - taps + r // L).astype(BF16)
    rows = lambda b, g, c: b * nc + c
    return pl.pallas_call(
        _ssd_body,
        grid=(batch, SSD_GROUPS // gps, nc),
        in_specs=[
            pl.BlockSpec((L, bw), lambda b, g, c: (rows(b, g, c), g)),
            pl.BlockSpec((L, bw), lambda b, g, c: (rows(b, g, c), x_blk + g)),
            pl.BlockSpec((L, bs), lambda b, g, c: (rows(b, g, c), b_blk + g)),
            pl.BlockSpec((L, bs), lambda b, g, c: (rows(b, g, c), c_blk + g)),
            pl.BlockSpec((L, gps * LANE), lambda b, g, c: (rows(b, g, c), g)),
            pl.BlockSpec((SSD_CONV, bw), lambda b, g, c: (0, g)),
            pl.BlockSpec((SSD_CONV, bs), lambda b, g, c: (0, cb_blk + g)),
            pl.BlockSpec((SSD_CONV, bs), lambda b, g, c: (0, cc_blk + g)),
            pl.BlockSpec((1, bw), lambda b, g, c: (0, g)),
            pl.BlockSpec((1, bs), lambda b, g, c: (0, cb_blk + g)),
            pl.BlockSpec((1, bs), lambda b, g, c: (0, cc_blk + g)),
            pl.BlockSpec((gps * SUBLANE, LANE), lambda b, g, c: (g, 0)),
            pl.BlockSpec((gps * SUBLANE, LANE), lambda b, g, c: (g, 0)),
            pl.BlockSpec((1, bw), lambda b, g, c: (0, g)),
            pl.BlockSpec((1, bw), lambda b, g, c: (0, g)),
            pl.BlockSpec((L, L), lambda b, g, c: (0, 0)),
            pl.BlockSpec((LANE, gw), lambda b, g, c: (0, 0)),
            pl.BlockSpec((taps * L, 2 * L), lambda b, g, c: (0, 0)),
        ],
        out_specs=pl.BlockSpec((L, bw), lambda b, g, c: (rows(b, g, c), g)),
        out_shape=jax.ShapeDtypeStruct((n, SSD_WIDTH), BF16),
        scratch_shapes=[
            pltpu.VMEM((gps, st, gw), F32),
            pltpu.VMEM((2, gps, L, gw + 2 * st), BF16),
        ],
        compiler_params=_params("parallel", "parallel", "arbitrary"),
        name="ssd",
    )(proj, proj, proj, proj, dt_raw, conv_w, conv_w, conv_w, conv_b, conv_b, conv_b,
      dtb, alog, dskip_e, onorm, tril, expand, shift)


def _sb_body(q_ref, k_ref, v_ref, uu_ref, o_ref):
    tq, kb = SB_QUERIES, SB_KEYS
    nd = tq // kb
    qi = pl.program_id(2)
    q = q_ref[...]
    uu = uu_ref[...]

    def sweep(base, masked, acc, carry):
        base = pl.multiple_of(base, tq)
        z_all = lax.dot_general(q, k_ref[pl.ds(base, tq), :], (((1,), (1,)), ((), ())),
                                preferred_element_type=F32)
        parts = []
        for d in range(nd):
            r0 = d * kb if masked else 0
            zf = z_all[r0:, d * kb:(d + 1) * kb]
            z = zf.astype(BF16)
            sp = jnp.log(1.0 + jnp.exp2((jnp.abs(zf) * (-LOG2E)).astype(BF16)))
            log_beta = jnp.minimum(z, 0.0) - sp
            log_keep = jnp.minimum(-z, 0.0) - sp
            if masked:
                rows_d = r0 + lax.broadcasted_iota(jnp.int32, (tq - r0, kb), 0)
                mask = (d * kb + lax.broadcasted_iota(jnp.int32, (tq - r0, kb), 1)) < rows_d
                log_keep = jnp.where(mask, log_keep, jnp.zeros_like(log_keep))
                log_beta = jnp.where(mask, log_beta, jnp.full_like(log_beta, MASKED_LOG))
            parts.append((r0, log_beta, _dot(log_keep, uu)))
        ws = [None] * nd
        for d in reversed(range(nd)):
            r0, log_beta, sums = parts[d]
            w = jnp.exp(log_beta.astype(F32) + sums[:, :kb] + carry[r0:]).astype(BF16)
            tot = carry[r0:] + sums[:, kb:]
            if r0:
                w = jnp.concatenate([jnp.zeros((r0, kb), BF16), w], axis=0)
                tot = jnp.concatenate([carry[:r0], tot], axis=0)
            ws[d], carry = w, tot
        acc = acc + _dot(jnp.concatenate(ws, axis=1), v_ref[pl.ds(base, tq), :])
        return acc, carry

    acc = jnp.zeros((tq, SB_HEAD_DIM), F32)
    carry = jnp.zeros((tq, kb), F32)
    acc, carry = sweep(qi * tq, True, acc, carry)
    odd = qi % 2
    acc, carry = lax.cond(odd == 1, lambda a, c: sweep((qi - 1) * tq, False, a, c), lambda a, c: (a, c),
                          acc, carry)

    def pair(i, ac):
        base = (qi - odd - 1 - 2 * i) * tq
        a, c = sweep(base, False, ac[0], ac[1])
        return sweep(base - tq, False, a, c)

    acc, carry = lax.fori_loop(0, qi // 2, pair, (acc, carry))
    o_ref[...] = acc.astype(BF16)


def _sb_attention(proj, batch, seq, q_blk, k_blk, v_blk):
    n = proj.shape[0]
    tq, kb, hd = SB_QUERIES, SB_KEYS, SB_HEAD_DIM
    nq = seq // tq
    r = lax.broadcasted_iota(jnp.int32, (kb, 2 * kb), 0)
    c = lax.broadcasted_iota(jnp.int32, (kb, 2 * kb), 1)
    uu = jnp.logical_or(c >= kb, r > c).astype(BF16)
    return pl.pallas_call(
        _sb_body,
        grid=(batch, SB_HEADS, nq),
        in_specs=[
            pl.BlockSpec((tq, hd), lambda b, h, i: (b * nq + i, q_blk + h)),
            pl.BlockSpec((seq, hd), lambda b, h, i: (b, k_blk + h)),
            pl.BlockSpec((seq, hd), lambda b, h, i: (b, v_blk + h)),
            pl.BlockSpec((kb, 2 * kb), lambda b, h, i: (0, 0)),
        ],
        out_specs=pl.BlockSpec((tq, hd), lambda b, h, i: (b * nq + i, h)),
        out_shape=jax.ShapeDtypeStruct((n, SB_WIDTH), BF16),
        compiler_params=_params("parallel", "parallel", "arbitrary"),
        name="sb_attention",
    )(proj, proj, proj, uu)


def _outproj_body(ya_ref, yb_ref, wa_ref, wb_ref, x_ref, o_ref):
    o_ref[...] = x_ref[...] + _dot(ya_ref[...], wa_ref[...]) + _dot(yb_ref[...], wb_ref[...])


def _outproj(y_ssd, y_sb, w_out, xf):
    n, d = xf.shape
    tm, tn = OUTPROJ_TM, OUTPROJ_TN
    ka, kb = y_ssd.shape[1], y_sb.shape[1]
    assert ka == kb and w_out.shape[0] == ka + kb
    return pl.pallas_call(
        _outproj_body,
        grid=(d // tn, n // tm),
        in_specs=[
            pl.BlockSpec((tm, ka), lambda j, i: (i, 0)),
            pl.BlockSpec((tm, kb), lambda j, i: (i, 0)),
            pl.BlockSpec((ka, tn), lambda j, i: (0, j)),
            pl.BlockSpec((kb, tn), lambda j, i: (1, j)),
            pl.BlockSpec((tm, tn), lambda j, i: (i, j)),
        ],
        out_specs=pl.BlockSpec((tm, tn), lambda j, i: (i, j)),
        out_shape=jax.ShapeDtypeStruct((n, d), F32),
        compiler_params=_params("parallel", "parallel"),
        name="outproj",
    )(y_ssd, y_sb, w_out, w_out, xf)


def _mlp_body(x_ref, g_ref, wu_ref, wd_ref, o_ref, xn_ref):
    f = pl.program_id(1)

    @pl.when(f == 0)
    def _():
        x = x_ref[...]
        xn_ref[...] = ((x * _rms_scale(x)) * g_ref[...]).astype(BF16)
        o_ref[...] = x

    u = jnp.maximum(_dot(xn_ref[...], wu_ref[...]), 0.0)
    o_ref[...] += _dot((u * u).astype(BF16), wd_ref[...])


def _mlp(xf, gain, w_up, w_down):
    n, d = xf.shape
    ff = w_up.shape[1]
    tm, tf = MLP_TM, MLP_TF
    return pl.pallas_call(
        _mlp_body,
        grid=(n // tm, ff // tf),
        in_specs=[
            pl.BlockSpec((tm, d), lambda i, f: (i, 0)),
            pl.BlockSpec((1, d), lambda i, f: (0, 0)),
            pl.BlockSpec((d, tf), lambda i, f: (0, f)),
            pl.BlockSpec((tf, d), lambda i, f: (f, 0)),
        ],
        out_specs=pl.BlockSpec((tm, d), lambda i, f: (i, 0)),
        out_shape=jax.ShapeDtypeStruct((n, d), F32),
        scratch_shapes=[pltpu.VMEM((tm, d), BF16)],
        compiler_params=_params("parallel", "arbitrary"),
        name="mlp",
    )(xf, gain, w_up, w_down)


def _pool_body(x_ref, halo_ref, g_ref, w_ref, b_ref, s_ref, o_ref, *, seq):
    tm = x_ref.shape[0]
    gch = w_ref.shape[1]
    t0 = (pl.program_id(0) * tm) % seq
    x = x_ref[...]
    gain = g_ref[...]
    h = (x * _rms_scale(x)) * gain
    halo = halo_ref[...]
    hh = (halo * _rms_scale(halo)) * gain
    hh = jnp.where(t0 == 0, 0.0, hh)
    ext = jnp.concatenate([hh, h], axis=0)
    pos = t0 + lax.broadcasted_iota(jnp.int32, (tm, 1), 0)
    for gi, win in enumerate(POOL_WINDOWS):
        sl = slice(gi * gch, (gi + 1) * gch)
        s = ext[:, sl]
        off, width = 0, 1
        while width < win:
            s = s[width:, :] + s[:-width, :]
            off += width
            width *= 2
        s = s[POOL_HALO - off:POOL_HALO - off + tm, :]
        count = jnp.minimum(pos + 1, win).astype(F32)
        dgrp = (s / count - h[:, sl]).astype(BF16)
        y = _dot(dgrp, w_ref[gi]) + b_ref[:, sl]
        o_ref[:, sl] = x[:, sl] + y * s_ref[:, sl]


def _pool(xf, gain, w, b, scale, seq):
    n, d = xf.shape
    tm = POOL_TM
    per = tm // POOL_HALO
    return pl.pallas_call(
        functools.partial(_pool_body, seq=seq),
        grid=(n // tm,),
        in_specs=[
            pl.BlockSpec((tm, d), lambda i: (i, 0)),
            pl.BlockSpec((POOL_HALO, d), lambda i: (jnp.maximum(i * per - 1, 0), 0)),
            pl.BlockSpec((1, d), lambda i: (0, 0)),
            pl.BlockSpec(w.shape, lambda i: (0, 0, 0)),
            pl.BlockSpec((1, d), lambda i: (0, 0)),
            pl.BlockSpec((1, d), lambda i: (0, 0)),
        ],
        out_specs=pl.BlockSpec((tm, d), lambda i: (i, 0)),
        out_shape=jax.ShapeDtypeStruct((n, d), F32),
        compiler_params=_params("parallel"),
        name="pool",
    )(xf, xf, gain, w, b, scale)


def _hybrid_layer(xf, batch, seq, norm, w_in, conv_w, conv_b, dt_bias, a_log, d_skip, out_norm,
                  q_norm, k_norm, w_out):
    d = xf.shape[1]
    conv_dim = conv_w.shape[1]
    dt_lo = SSD_WIDTH + conv_dim
    dt_hi = dt_lo + SSD_HEADS
    w_main = jnp.concatenate([w_in[:, :dt_lo], w_in[:, dt_hi:]], axis=1).astype(BF16)
    hpg = SSD_HEADS_PER_GROUP

    def per_group(v):
        v = v.reshape(v.shape[:-1] + (SSD_GROUPS, hpg))
        return jnp.pad(v, [(0, 0)] * (v.ndim - 1) + [(0, LANE - hpg)])

    w_dt = per_group(w_in[:, dt_lo:dt_hi]).reshape(d, SSD_GROUPS * LANE).astype(BF16)

    def group_rows(v):
        v = per_group(v.astype(F32))[:, None, :]
        return jnp.pad(v, [(0, 0), (0, SUBLANE - 1), (0, 0)]).reshape(SSD_GROUPS * SUBLANE, LANE)

    q_lo = dt_lo // INPROJ_TN
    sb_tiles = SB_WIDTH // INPROJ_TN
    proj, dt_raw = _inproj(xf, norm.reshape(1, d), w_main, w_dt,
                           q_norm.reshape(1, -1), k_norm.reshape(1, -1),
                           (q_lo, q_lo + sb_tiles), (q_lo + sb_tiles, q_lo + 2 * sb_tiles))
    y_ssd = _ssd(proj, dt_raw, conv_w, conv_b.reshape(1, -1), group_rows(dt_bias), group_rows(a_log),
                 jnp.repeat(d_skip.astype(F32), SSD_HEAD_DIM).reshape(1, -1), out_norm.reshape(1, -1),
                 batch, seq)
    q_blk = dt_lo // SB_HEAD_DIM
    y_sb = _sb_attention(proj, batch, seq, q_blk, q_blk + SB_HEADS, q_blk + 2 * SB_HEADS)
    return _outproj(y_ssd, y_sb, w_out, xf)


def kernel(x, hyb_norm, hyb_w_in, ssd_conv_w, ssd_conv_b, ssd_dt_bias, ssd_a_log, ssd_d, ssd_out_norm, sb_q_norm, sb_k_norm, hyb_w_out, pool_norm, pool_w, pool_b, pool_scale, mlp_norm, mlp_w_up, mlp_w_down):
    batch, seq, d = x.shape
    xf = x.reshape(batch * seq, d)
    for layer in range(mlp_norm.shape[0]):
        i = layer // 2
        if layer % 2 == 0:
            xf = _hybrid_layer(xf, batch, seq, hyb_norm[i], hyb_w_in[i], ssd_conv_w[i], ssd_conv_b[i],
                               ssd_dt_bias[i], ssd_a_log[i], ssd_d[i], ssd_out_norm[i],
                               sb_q_norm[i], sb_k_norm[i], _cast_layer(hyb_w_out, i))
        else:
            xf = _pool(xf, pool_norm[i].reshape(1, d), pool_w[i].astype(BF16), pool_b[i].reshape(1, d),
                       pool_scale[i].reshape(1, d), seq)
        xf = _mlp(xf, mlp_norm[layer].reshape(1, d), _cast_layer(mlp_w_up, layer),
                  _cast_layer(mlp_w_down, layer))
    return xf.reshape(batch, seq, d)
```
